```python
import jax, jax.numpy as jnp
from jax import lax
import numpy as np

D_MODEL = 1024
BATCH = 4
SEQ = 4096
DEPTH = 1
DEC_BATCH = 16
DEC_SEQ = 16
PAST_LEN = 1024

CHUNK = 64
MIX_WIDTH = D_MODEL
RET_HEADS = 4
RET_DK = 128
RET_DV = 128
RET_QK = RET_HEADS * RET_DK
RET_WIDTH = RET_HEADS * RET_DV
HG_HEADS = 4
HG_EXPAND = 128
HG_DV = 128
HG_FDIM = HG_HEADS * HG_EXPAND
HG_WIDTH = HG_HEADS * HG_DV
D_FF = 2816
CONV_W = 3
EPS = 1e-6
ROPE_BASE = 10000.0
IN_SIZES = (RET_QK, RET_QK, RET_WIDTH, RET_WIDTH, HG_FDIM, HG_FDIM, HG_WIDTH, HG_WIDTH)
IN_COLS = sum(IN_SIZES)
IN_OFFSETS = tuple(int(o) for o in np.cumsum(IN_SIZES)[:-1])

kernel_name = "retention_hgrn2_convffn_stream_step"


def rmsnorm(x, g):
    x32 = x.astype(jnp.float32)
    y = x32 * lax.rsqrt(jnp.mean(x32 * x32, axis=-1, keepdims=True) + EPS)
    return (y * g.astype(jnp.float32)).astype(x.dtype)


def rope(t, pos):
    half = t.shape[-1] // 2
    inv = 1.0 / (ROPE_BASE ** (jnp.arange(half, dtype=jnp.float32) / half))
    ang = pos.astype(jnp.float32)[:, None] * inv[None, :]
    cos, sin = jnp.cos(ang), jnp.sin(ang)
    t1, t2 = t[..., :half], t[..., half:]
    return jnp.concatenate([t1 * cos - t2 * sin, t1 * sin + t2 * cos], axis=-1)


def to_chunks(t, c):
    b, h, l, d = t.shape
    return t.reshape(b, h, l // c, c, d).transpose(2, 0, 1, 3, 4)


def from_chunks(t):
    n, b, h, c, d = t.shape
    return t.transpose(1, 2, 0, 3, 4).reshape(b, h, n * c, d)


def retention_chunkwise(q, k, v, state):
    L = q.shape[2]
    c = min(L, CHUNK)
    log_g = jnp.log(1.0 - 2.0 ** (-5.0 - jnp.arange(RET_HEADS, dtype=jnp.float32)))
    idx = jnp.arange(c)
    rel = idx[:, None] - idx[None, :]
    dmat = jnp.where(rel >= 0, jnp.exp(log_g[:, None, None] * jnp.maximum(rel, 0)), 0.0)
    q_decay = jnp.exp(log_g[:, None] * (idx + 1))[..., None]
    k_decay = jnp.exp(log_g[:, None] * (c - 1 - idx))[..., None]
    chunk_decay = jnp.exp(log_g * c)[:, None, None]

    def step(s, inp):
        qc, kc, vc = inp
        attn = jnp.einsum('bhid,bhjd->bhij', qc, kc) * dmat
        o = jnp.einsum('bhij,bhjv->bhiv', attn, vc) + jnp.einsum('bhid,bhdv->bhiv', qc * q_decay, s)
        s_new = s * chunk_decay + jnp.einsum('bhjd,bhjv->bhdv', kc * k_decay, vc)
        return s_new, o

    s_fin, o = lax.scan(step, state, (to_chunks(q, c), to_chunks(k, c), to_chunks(v, c)))
    return from_chunks(o), s_fin


def gated_chunkwise(q, k, v, g, state):
    L = q.shape[2]
    c = min(L, CHUNK)
    causal = (jnp.arange(c)[:, None] >= jnp.arange(c)[None, :])[:, :, None]

    def step(s, inp):
        qc, kc, vc, gc = inp
        b = jnp.cumsum(gc, axis=2)
        diff = b[:, :, :, None, :] - b[:, :, None, :, :]
        decay = jnp.exp(jnp.where(causal, diff, -jnp.inf))
        attn = jnp.einsum('bhik,bhjk,bhijk->bhij', qc, kc, decay)
        o = jnp.einsum('bhij,bhjv->bhiv', attn, vc) + jnp.einsum('bhik,bhkv->bhiv', qc * jnp.exp(b), s)
        b_last = b[:, :, -1]
        s_new = s * jnp.exp(b_last)[..., None] + jnp.einsum(
            'bhjk,bhjv->bhkv', kc * jnp.exp(b_last[:, :, None, :] - b), vc)
        return s_new, o

    s_fin, o = lax.scan(step, state, (to_chunks(q, c), to_chunks(k, c), to_chunks(v, c), to_chunks(g, c)))
    return from_chunks(o), s_fin


def hybrid_mixer(h, pos, s_ret, s_hg, w_in, ret_norm_g, hg_norm_g, lb, w_out):
    B, L, _ = h.shape
    proj = jnp.einsum('bld,dc->blc', h, w_in)
    rq, rk, rv, rg, hq, hf, hi, hgate = jnp.split(proj, IN_OFFSETS, axis=-1)

    def heads(t, n):
        return t.astype(jnp.float32).reshape(B, L, n, -1).transpose(0, 2, 1, 3)

    q_r = rope(heads(rq, RET_HEADS), pos)
    k_r = rope(heads(rk, RET_HEADS), pos) * (RET_DK ** -0.5)
    o_r, s_ret_new = retention_chunkwise(q_r, k_r, heads(rv, RET_HEADS), s_ret.astype(jnp.float32))
    o_r = o_r.transpose(0, 2, 1, 3)
    o_r = o_r * lax.rsqrt(jnp.mean(o_r * o_r, axis=-1, keepdims=True) + EPS)
    o_r = o_r.reshape(B, L, RET_WIDTH) * ret_norm_g.astype(jnp.float32)
    o_r = o_r * jax.nn.silu(rg.astype(jnp.float32))

    f = lb + (1.0 - lb) * jax.nn.sigmoid(hf.astype(jnp.float32))
    o_h, s_hg_new = gated_chunkwise(heads(hq, HG_HEADS), heads(1.0 - f, HG_HEADS),
                                    heads(hi, HG_HEADS), heads(jnp.log(f), HG_HEADS),
                                    s_hg.astype(jnp.float32))
    o_h = o_h.transpose(0, 2, 1, 3).reshape(B, L, HG_WIDTH)
    o_h = rmsnorm(o_h, hg_norm_g) * jax.nn.silu(hgate.astype(jnp.float32))

    merged = jnp.concatenate([o_r, o_h], axis=-1).astype(h.dtype)
    out = jnp.einsum('blc,cd->bld', merged, w_out)
    return out, s_ret_new.astype(h.dtype), s_hg_new.astype(h.dtype)


def conv_ffn(h, conv_state, w_up, conv_k, conv_b, w_down):
    L = h.shape[1]
    u = jnp.einsum('bld,df->blf', h, w_up)
    a, b = jnp.split(u, 2, axis=-1)
    a_ext = jnp.concatenate([conv_state.astype(a.dtype), a], axis=1)
    conv = conv_b
    for w in range(CONV_W):
        conv = conv + a_ext[:, w:w + L] * conv_k[w]
    y = jnp.einsum('blf,fd->bld', jax.nn.silu(conv) * b, w_down)
    return y, a_ext[:, -(CONV_W - 1):]


def trunk(x, pos, s_ret, s_hg, s_conv, w_in, ret_norm_g, hg_norm_g, lb_all, w_out,
          norm_mix, norm_ffn, w_up, conv_k, conv_b, w_down, norm_final):
    new_ret, new_hg, new_conv = [], [], []
    for l in range(DEPTH):
        h = rmsnorm(x, norm_mix[l])
        m, sr, sh = hybrid_mixer(h, pos, s_ret[l], s_hg[l], w_in[l], ret_norm_g[l],
                                 hg_norm_g[l], lb_all[l], w_out[l])
        x = x + m
        h = rmsnorm(x, norm_ffn[l])
        f, sc = conv_ffn(h, s_conv[l], w_up[l], conv_k[l], conv_b[l], w_down[l])
        x = x + f
        new_ret.append(sr)
        new_hg.append(sh)
        new_conv.append(sc)
    y = rmsnorm(x, norm_final)
    return y, jnp.stack(new_ret), jnp.stack(new_hg), jnp.stack(new_conv)


def setup_inputs(seed: int = 0) -> dict:
    key = jax.random.key(seed)
    ks = jax.random.split(key, 20)
    f32 = jnp.float32
    nrm = lambda k, shape, s: jax.random.normal(k, shape, f32) * s
    return {
        "x_prompt": nrm(ks[0], (BATCH, SEQ, D_MODEL), 1.0),
        "x_sample": nrm(ks[1], (DEC_BATCH, DEC_SEQ, D_MODEL), 1.0),
        "state_ret": nrm(ks[2], (DEPTH, DEC_BATCH, RET_HEADS, RET_DK, RET_DV), 0.3),
        "state_hgrn": nrm(ks[3], (DEPTH, DEC_BATCH, HG_HEADS, HG_EXPAND, HG_DV), 0.3),
        "cache_ffn_conv": nrm(ks[4], (DEPTH, DEC_BATCH, CONV_W - 1, D_FF), 1.0),
        "w_in": nrm(ks[5], (DEPTH, D_MODEL, IN_COLS), D_MODEL ** -0.5),
        "ret_norm_g": 1.0 + nrm(ks[6], (DEPTH, RET_WIDTH), 0.01),
        "hg_norm_g": 1.0 + nrm(ks[7], (DEPTH, HG_WIDTH), 0.01),
        "hg_lb_logits": nrm(ks[8], (DEPTH + 1, HG_FDIM), 0.1),
        "w_out": nrm(ks[9], (DEPTH, MIX_WIDTH, D_MODEL), MIX_WIDTH ** -0.5),
        "norm_mix": 1.0 + nrm(ks[10], (DEPTH, D_MODEL), 0.01),
        "norm_ffn": 1.0 + nrm(ks[11], (DEPTH, D_MODEL), 0.01),
        "w_up": nrm(ks[12], (DEPTH, D_MODEL, 2 * D_FF), D_MODEL ** -0.5),
        "conv_k": nrm(ks[13], (DEPTH, CONV_W, D_FF), CONV_W ** -0.5),
        "conv_b": nrm(ks[14], (DEPTH, D_FF), 0.01),
        "w_down": nrm(ks[15], (DEPTH, D_FF, D_MODEL), D_FF ** -0.5),
        "norm_final": 1.0 + nrm(ks[16], (D_MODEL,), 0.01),
    }


def reference(x_prompt, x_sample, state_ret, state_hgrn, cache_ffn_conv, w_in, ret_norm_g,
              hg_norm_g, hg_lb_logits, w_out, norm_mix, norm_ffn, w_up, conv_k, conv_b,
              w_down, norm_final):
    lb_all = jnp.cumsum(jax.nn.softmax(hg_lb_logits.astype(jnp.float32), axis=0), axis=0)[:DEPTH]
    weights = (w_in, ret_norm_g, hg_norm_g, lb_all, w_out, norm_mix, norm_ffn,
               w_up, conv_k, conv_b, w_down, norm_final)

    pos_p = jnp.arange(SEQ, dtype=jnp.int32)
    zr = jnp.zeros((DEPTH, BATCH, RET_HEADS, RET_DK, RET_DV), jnp.float32)
    zh = jnp.zeros((DEPTH, BATCH, HG_HEADS, HG_EXPAND, HG_DV), jnp.float32)
    zc = jnp.zeros((DEPTH, BATCH, CONV_W - 1, D_FF), x_prompt.dtype)
    y_prompt, ret_p, hg_p, conv_p = trunk(x_prompt, pos_p, zr, zh, zc, *weights)

    pos_s = PAST_LEN + jnp.arange(x_sample.shape[1], dtype=jnp.int32)
    y_sample, ret_s, hg_s, conv_s = trunk(x_sample, pos_s, state_ret, state_hgrn,
                                          cache_ffn_conv, *weights)
    return (y_prompt, y_sample, ret_p, hg_p, conv_p, ret_s, hg_s, conv_s)
```

```python
import functools
import math

import jax
import jax.numpy as jnp
from jax import lax
from jax.experimental import pallas as pl
from jax.experimental.pallas import tpu as pltpu

D_MODEL = 1024
N_HEADS = 4
D_HEAD = 128
GROUP_W = N_HEADS * D_HEAD
IN_COLS = 8 * GROUP_W
D_FF = 2816
CONV_W = 3
EPS = 1e-6
ROPE_BASE = 10000.0
PAST_LEN = 1024

_RQ, _RK, _RV, _RG, _HQ, _HF, _HI, _HGATE = (i * GROUP_W for i in range(8))

_VMEM_LIMIT_BYTES = 56 * 1024 * 1024
_MXU_COLS = 256

_F32 = jnp.float32
_BF16 = jnp.bfloat16


def _sigmoid(x):
    return 1.0 / (1.0 + jnp.exp(-x))


def _silu(x):
    return x * _sigmoid(x)


def _dot(a, b):
    return jnp.dot(a, b, preferred_element_type=_F32)


def _dot_nt(a, b):
    return lax.dot_general(a, b, (((1,), (1,)), ((), ())), preferred_element_type=_F32)


def _dot_tn(a, b):
    return lax.dot_general(a, b, (((0,), (0,)), ((), ())), preferred_element_type=_F32)


def _split3(x):
    x1 = x.astype(_BF16)
    r1 = x - x1.astype(_F32)
    x2 = r1.astype(_BF16)
    r2 = r1 - x2.astype(_F32)
    return x1, x2, r2.astype(_BF16)


def _levels(t):
    out = []
    s = t // 2
    while s >= 1:
        out.append(s)
        s //= 2
    return out


def _const_spec(shape):
    return pl.BlockSpec(shape, lambda bi, ti: (0,) * len(shape), pipeline_mode=pl.Buffered(1))


def _mixer_kernel(x_ref, cos_ref, sin_ref, sret0_ref, shg0_ref, w_in_ref, w_out_ref,
                  nmix_ref, retg_ref, hgg_ref, lbl_ref,
                  x1_ref, sret_ref, shg_ref,
                  proj_sc, merged_sc, oh_sc, qs_sc, ks_sc, sret_sc, shgt_sc,
                  *, nb, t):
    step = pl.program_id(1)
    last = pl.num_programs(1) - 1
    levels = _levels(t)

    @pl.when(step == 0)
    def _():
        sret_sc[...] = sret0_ref[...]
        for i in range(nb):
            for hd in range(N_HEADS):
                shgt_sc[i, hd] = shg0_ref[i, hd].T

    x = x_ref[...]
    h = x * lax.rsqrt(jnp.mean(x * x, axis=-1, keepdims=True) + EPS) * nmix_ref[...]
    hb = h.astype(_BF16)
    for c in range(8):
        cols = slice(c * GROUP_W, (c + 1) * GROUP_W)
        proj_sc[:, cols] = _dot(hb, w_in_ref[:, cols])

    l0 = lbl_ref[0:1, :]
    l1 = lbl_ref[1:2, :]
    lmax = jnp.maximum(l0, l1)
    e0 = jnp.exp(l0 - lmax)
    e1 = jnp.exp(l1 - lmax)
    lb = e0 / (e0 + e1)

    cos = cos_ref[...]
    sin = sin_ref[...]

    ii = lax.broadcasted_iota(jnp.int32, (t, t), 0)
    jj = lax.broadcasted_iota(jnp.int32, (t, t), 1)
    rel = ii - jj
    relf = jnp.maximum(rel, 0).astype(_F32)
    causal = rel >= 0
    tri = jnp.where(causal, 1.0, 0.0).astype(_BF16)
    xor_f = jnp.bitwise_xor(ii, jj).astype(_F32)
    pair_level = (pltpu.bitcast(xor_f, jnp.int32) >> 23) - 127
    pair_level = jnp.where(rel > 0, pair_level, jnp.where(rel == 0, -1, -2))
    tpos = lax.broadcasted_iota(jnp.int32, (t, 1), 0)
    tposf = tpos.astype(_F32)
    ret_log_g = [math.log(1.0 - 2.0 ** (-5.0 - hd)) for hd in range(N_HEADS)]
    ret_dmat = [jnp.where(causal, jnp.exp(lg * relf), 0.0) for lg in ret_log_g]

    def mix_one(i):
        r0 = i * t if isinstance(i, int) else pl.multiple_of(i * t, t)
        rows = pl.ds(r0, t)

        for hd in range(N_HEADS):
            log_g = ret_log_g[hd]
            hc = slice(hd * D_HEAD, (hd + 1) * D_HEAD)
            q = proj_sc[rows, _RQ + hd * D_HEAD:_RQ + (hd + 1) * D_HEAD]
            k = proj_sc[rows, _RK + hd * D_HEAD:_RK + (hd + 1) * D_HEAD]
            v = proj_sc[rows, _RV + hd * D_HEAD:_RV + (hd + 1) * D_HEAD]
            gate = proj_sc[rows, _RG + hd * D_HEAD:_RG + (hd + 1) * D_HEAD]
            q = q * cos + pltpu.roll(q, D_HEAD // 2, 1) * sin
            k = (k * cos + pltpu.roll(k, D_HEAD // 2, 1) * sin) * (D_HEAD ** -0.5)
            vb = v.astype(_BF16)
            attn = _dot_nt(q.astype(_BF16), k.astype(_BF16)) * ret_dmat[hd]
            s_old = sret_sc[i, hd]
            q_dec = jnp.exp(log_g * (tposf + 1.0))
            k_dec = jnp.exp(log_g * (float(t - 1) - tposf))
            o = _dot(attn.astype(_BF16), vb) + _dot((q * q_dec).astype(_BF16), s_old.astype(_BF16))
            sret_sc[i, hd] = s_old * math.exp(log_g * t) + _dot_tn((k * k_dec).astype(_BF16), vb)
            o = o * lax.rsqrt(jnp.mean(o * o, axis=-1, keepdims=True) + EPS)
            o = o * retg_ref[:, hc] * _silu(gate)
            merged_sc[rows, hc] = o.astype(_BF16)

        hf = proj_sc[rows, _HF:_HF + GROUP_W]
        f = lb + (1.0 - lb) * _sigmoid(hf)
        g = jnp.log(f)
        kk = 1.0 - f
        hq = proj_sc[rows, _HQ:_HQ + GROUP_W]
        g1, g2, g3 = _split3(g)
        b = _dot(tri, g1) + _dot(tri, g2) + _dot(tri, g3)

        qs_sc[0, rows, :] = hq.astype(_BF16)
        ks_sc[0, rows, :] = kk.astype(_BF16)
        for li, s in enumerate(levels):
            if 2 * s >= 8:
                nblk = t // (2 * s)
                b3 = b.reshape(nblk, 2 * s, GROUP_W)
                ref_row = b3[:, s - 1:s, :]
                e = jnp.exp(-jnp.abs(b3 - ref_row)).reshape(t, GROUP_W)
            elif s == 2:
                g_prev = pltpu.roll(g, 1, 0)
                g_next = pltpu.roll(g, t - 1, 0)
                m4 = jnp.bitwise_and(tpos, 3)
                d = jnp.where(m4 == 0, g_next, jnp.where(m4 == 1, 0.0, jnp.where(m4 == 2, g, g + g_prev)))
                e = jnp.exp(d)
            else:
                e = jnp.exp(jnp.where(jnp.bitwise_and(tpos, 1) == 1, g, 0.0))
            qs_sc[li + 1, rows, :] = (hq * e).astype(_BF16)
            ks_sc[li + 1, rows, :] = (kk * e).astype(_BF16)

        for hd in range(N_HEADS):
            hc = slice(hd * D_HEAD, (hd + 1) * D_HEAD)
            a = jnp.where(pair_level == -1, _dot_nt(qs_sc[0, rows, hc], ks_sc[0, rows, hc]), 0.0)
            for li, s in enumerate(levels):
                p = _dot_nt(qs_sc[li + 1, rows, hc], ks_sc[li + 1, rows, hc])
                a = a + jnp.where(pair_level == int(math.log2(s)), p, 0.0)
            bh = b[:, hc]
            vb = proj_sc[rows, _HI + hd * D_HEAD:_HI + (hd + 1) * D_HEAD].astype(_BF16)
            st_old = shgt_sc[i, hd]
            o = _dot(a.astype(_BF16), vb) + _dot_nt((hq[:, hc] * jnp.exp(bh)).astype(_BF16),
                                                    st_old.astype(_BF16))
            b_last = bh[t - 1:t, :]
            ke = kk[:, hc] * jnp.exp(b_last - bh)
            shgt_sc[i, hd] = st_old * jnp.exp(b_last) + _dot_tn(vb, ke.astype(_BF16))
            oh_sc[rows, hc] = o

        oh = oh_sc[rows, :]
        oh = oh * lax.rsqrt(jnp.mean(oh * oh, axis=-1, keepdims=True) + EPS) * hgg_ref[...]
        oh = oh * _silu(proj_sc[rows, _HGATE:_HGATE + GROUP_W])
        merged_sc[rows, GROUP_W:2 * GROUP_W] = oh.astype(_BF16)

    if nb == 1:
        mix_one(0)
    else:
        def body(i, carry):
            mix_one(i)
            return carry
        lax.fori_loop(0, nb, body, 0)

    x1_ref[...] = x_ref[...] + _dot(merged_sc[...], w_out_ref[...])

    @pl.when(step == last)
    def _():
        sret_ref[...] = sret_sc[...]
        for i in range(nb):
            for hd in range(N_HEADS):
                shg_ref[i, hd] = shgt_sc[i, hd].T


def _mixer(x2d, cos2, sin2, sret0, shg0, w_in, w_out, nmix, retg, hgg, lbl, *, nseq, seqlen, nb, t):
    nbt = nseq // nb
    nt = seqlen // t
    r = nb * t
    nlev = len(_levels(t)) + 1
    state_spec = pl.BlockSpec((nb, N_HEADS, D_HEAD, D_HEAD), lambda bi, ti: (bi, 0, 0, 0))
    return pl.pallas_call(
        functools.partial(_mixer_kernel, nb=nb, t=t),
        grid=(nbt, nt),
        in_specs=[
            pl.BlockSpec((r, D_MODEL), lambda bi, ti: (bi * nt + ti, 0)),
            pl.BlockSpec((t, D_HEAD), lambda bi, ti: (ti, 0)),
            pl.BlockSpec((t, D_HEAD), lambda bi, ti: (ti, 0)),
            state_spec,
            state_spec,
            _const_spec((D_MODEL, IN_COLS)),
            _const_spec((2 * GROUP_W, D_MODEL)),
            _const_spec((1, D_MODEL)),
            _const_spec((1, GROUP_W)),
            _const_spec((1, GROUP_W)),
            _const_spec((2, GROUP_W)),
        ],
        out_specs=[
            pl.BlockSpec((r, D_MODEL), lambda bi, ti: (bi * nt + ti, 0)),
            state_spec,
            state_spec,
        ],
        out_shape=[
            jax.ShapeDtypeStruct((nseq * seqlen, D_MODEL), _F32),
            jax.ShapeDtypeStruct((nseq, N_HEADS, D_HEAD, D_HEAD), _F32),
            jax.ShapeDtypeStruct((nseq, N_HEADS, D_HEAD, D_HEAD), _F32),
        ],
        scratch_shapes=[
            pltpu.VMEM((r, IN_COLS), _F32),
            pltpu.VMEM((r, 2 * GROUP_W), _BF16),
            pltpu.VMEM((r, GROUP_W), _F32),
            pltpu.VMEM((nlev, r, GROUP_W), _BF16),
            pltpu.VMEM((nlev, r, GROUP_W), _BF16),
            pltpu.VMEM((nb, N_HEADS, D_HEAD, D_HEAD), _F32),
            pltpu.VMEM((nb, N_HEADS, D_HEAD, D_HEAD), _F32),
        ],
        compiler_params=pltpu.CompilerParams(
            dimension_semantics=("arbitrary", "arbitrary"),
            vmem_limit_bytes=_VMEM_LIMIT_BYTES),
        name="mixer",
    )(x2d, cos2, sin2, sret0, shg0, w_in, w_out, nmix, retg, hgg, lbl)


def _ff_chunks():
    step = 2 * _MXU_COLS
    return [(c0, min(c0 + step, D_FF)) for c0 in range(0, D_FF, step)]


def _ffn_kernel(x1_ref, cache_ref, w_up_ref, w_down_ref, nffn_ref, ck_ref, cb_ref, nfin_ref,
                y_ref, cache_out_ref,
                act_sc, carry_sc, *, nb, t):
    step = pl.program_id(1)
    last = pl.num_programs(1) - 1
    r = nb * t

    @pl.when(step == 0)
    def _():
        carry_sc[...] = cache_ref[...]

    x1 = x1_ref[...]
    h = x1 * lax.rsqrt(jnp.mean(x1 * x1, axis=-1, keepdims=True) + EPS) * nffn_ref[...]
    hb = h.astype(_BF16)
    tpos = lax.broadcasted_iota(jnp.int32, (nb, t, 1), 1)

    for c0, c1 in _ff_chunks():
        w = c1 - c0
        cols = slice(c0, c1)
        a3 = _dot(hb, w_up_ref[:, cols]).reshape(nb, t, w)
        bgate = _dot(hb, w_up_ref[:, D_FF + c0:D_FF + c1])
        c_m2 = carry_sc[:, 0:1, cols]
        c_m1 = carry_sc[:, 1:2, cols]
        a_m1 = jnp.where(tpos == 0, c_m1, pltpu.roll(a3, 1, 1))
        a_m2 = jnp.where(tpos == 0, c_m2, jnp.where(tpos == 1, c_m1, pltpu.roll(a3, 2, 1)))
        conv = cb_ref[:, cols] + a_m2 * ck_ref[0:1, cols] + a_m1 * ck_ref[1:2, cols] + a3 * ck_ref[2:3, cols]
        act_sc[:, cols] = (_silu(conv).reshape(r, w) * bgate).astype(_BF16)
        carry_sc[:, :, cols] = a3[:, t - 2:t, :]

    x2 = x1 + _dot(act_sc[...], w_down_ref[...])
    y_ref[...] = x2 * lax.rsqrt(jnp.mean(x2 * x2, axis=-1, keepdims=True) + EPS) * nfin_ref[...]

    @pl.when(step == last)
    def _():
        cache_out_ref[...] = carry_sc[...]


def _ffn(x1_2d, cache, w_up, w_down, nffn, ck, cb, nfin, *, nseq, seqlen, nb, t):
    nbt = nseq // nb
    nt = seqlen // t
    r = nb * t
    cache_spec = pl.BlockSpec((nb, CONV_W - 1, D_FF), lambda bi, ti: (bi, 0, 0))
    return pl.pallas_call(
        functools.partial(_ffn_kernel, nb=nb, t=t),
        grid=(nbt, nt),
        in_specs=[
            pl.BlockSpec((r, D_MODEL), lambda bi, ti: (bi * nt + ti, 0)),
            cache_spec,
            _const_spec((D_MODEL, 2 * D_FF)),
            _const_spec((D_FF, D_MODEL)),
            _const_spec((1, D_MODEL)),
            _const_spec((CONV_W, D_FF)),
            _const_spec((1, D_FF)),
            _const_spec((1, D_MODEL)),
        ],
        out_specs=[
            pl.BlockSpec((r, D_MODEL), lambda bi, ti: (bi * nt + ti, 0)),
            cache_spec,
        ],
        out_shape=[
            jax.ShapeDtypeStruct((nseq * seqlen, D_MODEL), _F32),
            jax.ShapeDtypeStruct((nseq, CONV_W - 1, D_FF), _F32),
        ],
        scratch_shapes=[
            pltpu.VMEM((r, D_FF), _BF16),
            pltpu.VMEM((nb, CONV_W - 1, D_FF), _F32),
        ],
        compiler_params=pltpu.CompilerParams(
            dimension_semantics=("arbitrary", "arbitrary"),
            vmem_limit_bytes=_VMEM_LIMIT_BYTES),
        name="convffn",
    )(x1_2d, cache, w_up, w_down, nffn, ck, cb, nfin)


def _rope_tables(pos):
    half = D_HEAD // 2
    inv = 1.0 / (ROPE_BASE ** (jnp.arange(half, dtype=_F32) / half))
    ang = pos.astype(_F32)[:, None] * inv[None, :]
    cos, sin = jnp.cos(ang), jnp.sin(ang)
    return jnp.concatenate([cos, cos], axis=-1), jnp.concatenate([-sin, sin], axis=-1)


def _trunk(x, pos, s_ret, s_hg, s_conv, weights, *, nb, t_mix, t_ffn):
    (w_in, w_out, w_up, w_down, nmix, retg, hgg, lbl, nffn, ck, cb, nfin) = weights
    nseq, seqlen, _ = x.shape
    cos2, sin2 = _rope_tables(pos)
    x2d = x.reshape(nseq * seqlen, D_MODEL)
    x1, ret_new, hg_new = _mixer(x2d, cos2, sin2, s_ret, s_hg, w_in, w_out, nmix, retg, hgg, lbl,
                                 nseq=nseq, seqlen=seqlen, nb=nb, t=t_mix)
    y, conv_new = _ffn(x1, s_conv, w_up, w_down, nffn, ck, cb, nfin,
                       nseq=nseq, seqlen=seqlen, nb=nb, t=t_ffn)
    return y.reshape(nseq, seqlen, D_MODEL), ret_new[None], hg_new[None], conv_new[None]


def kernel(x_prompt, x_sample, state_ret, state_hgrn, cache_ffn_conv, w_in, ret_norm_g, hg_norm_g,
           hg_lb_logits, w_out, norm_mix, norm_ffn, w_up, conv_k, conv_b, w_down, norm_final):
    assert w_in.shape[0] == 1, "single-layer trunk"
    weights = (
        w_in[0].astype(_BF16), w_out[0].astype(_BF16), w_up[0].astype(_BF16), w_down[0].astype(_BF16),
        norm_mix[0][None].astype(_F32), ret_norm_g[0][None].astype(_F32), hg_norm_g[0][None].astype(_F32),
        hg_lb_logits.astype(_F32), norm_ffn[0][None].astype(_F32), conv_k[0].astype(_F32),
        conv_b[0][None].astype(_F32), norm_final[None].astype(_F32),
    )
    nseq_p, seq_p, _ = x_prompt.shape
    nseq_s, seq_s, _ = x_sample.shape

    zeros_state = jnp.zeros((nseq_p, N_HEADS, D_HEAD, D_HEAD), _F32)
    zeros_conv = jnp.zeros((nseq_p, CONV_W - 1, D_FF), _F32)
    y_p, ret_p, hg_p, conv_p = _trunk(
        x_prompt, jnp.arange(seq_p, dtype=jnp.int32), zeros_state, zeros_state, zeros_conv, weights,
        nb=1, t_mix=256, t_ffn=256)

    pos_s = PAST_LEN + jnp.arange(seq_s, dtype=jnp.int32)
    y_s, ret_s, hg_s, conv_s = _trunk(
        x_sample, pos_s, state_ret[0].astype(_F32), state_hgrn[0].astype(_F32),
        cache_ffn_conv[0].astype(_F32), weights,
        nb=nseq_s, t_mix=seq_s, t_ffn=seq_s)
    return (y_p, y_s, ret_p, hg_p, conv_p, ret_s, hg_s, conv_s)
```

```python
import functools
import math

import jax
import jax.numpy as jnp
from jax import lax
from jax.experimental import pallas as pl
from jax.experimental.pallas import tpu as pltpu

D_MODEL = 1024
N_HEADS = 4
D_HEAD = 128
GROUP_W = N_HEADS * D_HEAD
IN_COLS = 8 * GROUP_W
D_FF = 2816
CONV_W = 3
EPS = 1e-6
ROPE_BASE = 10000.0
PAST_LEN = 1024

_RQ, _RK, _RV, _RG, _HQ, _HF, _HI, _HGATE = (i * GROUP_W for i in range(8))

_VMEM_LIMIT_BYTES = 58 * 1024 * 1024
_MXU_COLS = 256
_PROMPT_TILE = 256

_F32 = jnp.float32
_BF16 = jnp.bfloat16


def _sigmoid(x):
    return 1.0 / (1.0 + jnp.exp(-x))


def _silu(x):
    return x * _sigmoid(x)


def _dot(a, b):
    return jnp.dot(a, b, preferred_element_type=_F32)


def _dot_nt(a, b):
    return lax.dot_general(a, b, (((1,), (1,)), ((), ())), preferred_element_type=_F32)


def _dot_tn(a, b):
    return lax.dot_general(a, b, (((0,), (0,)), ((), ())), preferred_element_type=_F32)


def _split3(x):
    x1 = x.astype(_BF16)
    r1 = x - x1.astype(_F32)
    x2 = r1.astype(_BF16)
    r2 = r1 - x2.astype(_F32)
    return x1, x2, r2.astype(_BF16)


def _levels(t):
    out = []
    s = t // 2
    while s >= 1:
        out.append(s)
        s //= 2
    return out


def _rmsnorm(x, g):
    return x * lax.rsqrt(jnp.mean(x * x, axis=-1, keepdims=True) + EPS) * g


def _mixer_stage(x_ref, cos_ref, sin_ref, w_in_ref, w_out_ref, nmix_ref, retg_ref, hgg_ref, lbl_ref,
                 proj_sc, merged_sc, oh_sc, qs_sc, ks_sc, sret_sc, shgt_sc, *, nb, t):
    levels = _levels(t)

    x = x_ref[...]
    hb = _rmsnorm(x, nmix_ref[...]).astype(_BF16)
    for c in range(8):
        cols = slice(c * GROUP_W, (c + 1) * GROUP_W)
        proj_sc[:, cols] = _dot(hb, w_in_ref[:, cols])

    l0 = lbl_ref[0:1, :]
    l1 = lbl_ref[1:2, :]
    lmax = jnp.maximum(l0, l1)
    e0 = jnp.exp(l0 - lmax)
    e1 = jnp.exp(l1 - lmax)
    lb = e0 / (e0 + e1)

    cos = cos_ref[...]
    sin = sin_ref[...]

    ii = lax.broadcasted_iota(jnp.int32, (t, t), 0)
    jj = lax.broadcasted_iota(jnp.int32, (t, t), 1)
    rel = ii - jj
    relf = jnp.maximum(rel, 0).astype(_F32)
    causal = rel >= 0
    tri = jnp.where(causal, 1.0, 0.0).astype(_BF16)
    xor_f = jnp.bitwise_xor(ii, jj).astype(_F32)
    pair_level = (pltpu.bitcast(xor_f, jnp.int32) >> 23) - 127
    pair_level = jnp.where(rel > 0, pair_level, jnp.where(rel == 0, -1, -2))
    tpos = lax.broadcasted_iota(jnp.int32, (t, 1), 0)
    tposf = tpos.astype(_F32)
    ret_log_g = [math.log(1.0 - 2.0 ** (-5.0 - hd)) for hd in range(N_HEADS)]
    ret_dmat = [jnp.where(causal, jnp.exp(lg * relf), 0.0) for lg in ret_log_g]

    def mix_one(i):
        r0 = i * t if isinstance(i, int) else pl.multiple_of(i * t, t)
        rows = pl.ds(r0, t)

        for hd in range(N_HEADS):
            log_g = ret_log_g[hd]
            hc = slice(hd * D_HEAD, (hd + 1) * D_HEAD)
            q = proj_sc[rows, _RQ + hd * D_HEAD:_RQ + (hd + 1) * D_HEAD]
            k = proj_sc[rows, _RK + hd * D_HEAD:_RK + (hd + 1) * D_HEAD]
            v = proj_sc[rows, _RV + hd * D_HEAD:_RV + (hd + 1) * D_HEAD]
            gate = proj_sc[rows, _RG + hd * D_HEAD:_RG + (hd + 1) * D_HEAD]
            q = q * cos + pltpu.roll(q, D_HEAD // 2, 1) * sin
            k = (k * cos + pltpu.roll(k, D_HEAD // 2, 1) * sin) * (D_HEAD ** -0.5)
            vb = v.astype(_BF16)
            attn = _dot_nt(q.astype(_BF16), k.astype(_BF16)) * ret_dmat[hd]
            s_old = sret_sc[i, hd]
            q_dec = jnp.exp(log_g * (tposf + 1.0))
            k_dec = jnp.exp(log_g * (float(t - 1) - tposf))
            o = _dot(attn.astype(_BF16), vb) + _dot((q * q_dec).astype(_BF16), s_old.astype(_BF16))
            sret_sc[i, hd] = s_old * math.exp(log_g * t) + _dot_tn((k * k_dec).astype(_BF16), vb)
            o = o * lax.rsqrt(jnp.mean(o * o, axis=-1, keepdims=True) + EPS)
            o = o * retg_ref[:, hc] * _silu(gate)
            merged_sc[rows, hc] = o.astype(_BF16)

        hf = proj_sc[rows, _HF:_HF + GROUP_W]
        f = lb + (1.0 - lb) * _sigmoid(hf)
        g = jnp.log(f)
        kk = 1.0 - f
        hq = proj_sc[rows, _HQ:_HQ + GROUP_W]
        g1, g2, g3 = _split3(g)
        b = _dot(tri, g1) + _dot(tri, g2) + _dot(tri, g3)

        qs_sc[0, rows, :] = hq.astype(_BF16)
        ks_sc[0, rows, :] = kk.astype(_BF16)
        for li, s in enumerate(levels):
            if 2 * s >= 8:
                nblk = t // (2 * s)
                b3 = b.reshape(nblk, 2 * s, GROUP_W)
                ref_row = b3[:, s - 1:s, :]
                e = jnp.exp(-jnp.abs(b3 - ref_row)).reshape(t, GROUP_W)
            elif s == 2:
                g_prev = pltpu.roll(g, 1, 0)
                g_next = pltpu.roll(g, t - 1, 0)
                m4 = jnp.bitwise_and(tpos, 3)
                d = jnp.where(m4 == 0, g_next, jnp.where(m4 == 1, 0.0, jnp.where(m4 == 2, g, g + g_prev)))
                e = jnp.exp(d)
            else:
                e = jnp.exp(jnp.where(jnp.bitwise_and(tpos, 1) == 1, g, 0.0))
            qs_sc[li + 1, rows, :] = (hq * e).astype(_BF16)
            ks_sc[li + 1, rows, :] = (kk * e).astype(_BF16)

        for hd in range(N_HEADS):
            hc = slice(hd * D_HEAD, (hd + 1) * D_HEAD)
            a = jnp.where(pair_level == -1, _dot_nt(qs_sc[0, rows, hc], ks_sc[0, rows, hc]), 0.0)
            for li, s in enumerate(levels):
                p = _dot_nt(qs_sc[li + 1, rows, hc], ks_sc[li + 1, rows, hc])
                a = a + jnp.where(pair_level == int(math.log2(s)), p, 0.0)
            bh = b[:, hc]
            vb = proj_sc[rows, _HI + hd * D_HEAD:_HI + (hd + 1) * D_HEAD].astype(_BF16)
            st_old = shgt_sc[i, hd]
            o = _dot(a.astype(_BF16), vb) + _dot_nt((hq[:, hc] * jnp.exp(bh)).astype(_BF16),
                                                    st_old.astype(_BF16))
            b_last = bh[t - 1:t, :]
            ke = kk[:, hc] * jnp.exp(b_last - bh)
            shgt_sc[i, hd] = st_old * jnp.exp(b_last) + _dot_tn(vb, ke.astype(_BF16))
            oh_sc[rows, hc] = o

        oh = _rmsnorm(oh_sc[rows, :], hgg_ref[...])
        oh = oh * _silu(proj_sc[rows, _HGATE:_HGATE + GROUP_W])
        merged_sc[rows, GROUP_W:2 * GROUP_W] = oh.astype(_BF16)

    if nb == 1:
        mix_one(0)
    else:
        def body(i, carry):
            mix_one(i)
            return carry
        lax.fori_loop(0, nb, body, 0)

    return x_ref[...] + _dot(merged_sc[...], w_out_ref[...])


def _load_states(sret0_ref, shg0_ref, sret_sc, shgt_sc, nb):
    sret_sc[...] = sret0_ref[...]
    for i in range(nb):
        for hd in range(N_HEADS):
            shgt_sc[i, hd] = shg0_ref[i, hd].T


def _store_states(sret_ref, shg_ref, sret_sc, shgt_sc, nb):
    sret_ref[...] = sret_sc[...]
    for i in range(nb):
        for hd in range(N_HEADS):
            shg_ref[i, hd] = shgt_sc[i, hd].T


def _mixer_scratch(nb, t):
    r = nb * t
    nlev = len(_levels(t)) + 1
    return [
        pltpu.VMEM((r, IN_COLS), _F32),
        pltpu.VMEM((r, 2 * GROUP_W), _BF16),
        pltpu.VMEM((r, GROUP_W), _F32),
        pltpu.VMEM((nlev, r, GROUP_W), _BF16),
        pltpu.VMEM((nlev, r, GROUP_W), _BF16),
        pltpu.VMEM((nb, N_HEADS, D_HEAD, D_HEAD), _F32),
        pltpu.VMEM((nb, N_HEADS, D_HEAD, D_HEAD), _F32),
    ]


def _ff_chunks():
    step = 2 * _MXU_COLS
    return [(c0, min(c0 + step, D_FF)) for c0 in range(0, D_FF, step)]


def _ffn_stage(x1, w_up_ref, w_down_ref, nffn_ref, ck_ref, cb_ref, nfin_ref, act_sc, carry_sc, *, nb, t):
    r = nb * t
    hb = _rmsnorm(x1, nffn_ref[...]).astype(_BF16)
    tpos = lax.broadcasted_iota(jnp.int32, (nb, t, 1), 1)

    for c0, c1 in _ff_chunks():
        w = c1 - c0
        cols = slice(c0, c1)
        a3 = _dot(hb, w_up_ref[:, cols]).reshape(nb, t, w)
        bgate = _dot(hb, w_up_ref[:, D_FF + c0:D_FF + c1])
        c_m2 = carry_sc[:, 0:1, cols]
        c_m1 = carry_sc[:, 1:2, cols]
        a_m1 = jnp.where(tpos == 0, c_m1, pltpu.roll(a3, 1, 1))
        a_m2 = jnp.where(tpos == 0, c_m2, jnp.where(tpos == 1, c_m1, pltpu.roll(a3, 2, 1)))
        conv = cb_ref[:, cols] + a_m2 * ck_ref[0:1, cols] + a_m1 * ck_ref[1:2, cols] + a3 * ck_ref[2:3, cols]
        act_sc[:, cols] = (_silu(conv).reshape(r, w) * bgate).astype(_BF16)
        carry_sc[:, :, cols] = a3[:, t - 2:t, :]

    x2 = x1 + _dot(act_sc[...], w_down_ref[...])
    return _rmsnorm(x2, nfin_ref[...])


def _ffn_scratch(nb, t):
    return [
        pltpu.VMEM((nb * t, D_FF), _BF16),
        pltpu.VMEM((nb, CONV_W - 1, D_FF), _F32),
    ]


def _prompt_kernel(x_ref, cos_ref, sin_ref, sret0_ref, shg0_ref, cache_ref,
                   w_in_ref, w_out_ref, w_up_ref, w_down_ref,
                   nmix_ref, retg_ref, hgg_ref, lbl_ref, nffn_ref, ck_ref, cb_ref, nfin_ref,
                   y_ref, sret_ref, shg_ref, cache_out_ref,
                   proj_sc, merged_sc, oh_sc, qs_sc, ks_sc, sret_sc, shgt_sc, act_sc, carry_sc, x1_sc,
                   *, t, nt):
    g = pl.program_id(0)
    tm = lax.rem(g, nt)
    tf = lax.rem(g + (nt - 1), nt)

    @pl.when(tm == 0)
    def _():
        _load_states(sret0_ref, shg0_ref, sret_sc, shgt_sc, 1)

    @pl.when(tf == 0)
    def _():
        carry_sc[...] = cache_ref[...]

    @pl.when(g == 0)
    def _():
        x1_sc[...] = jnp.zeros_like(x1_sc)
        carry_sc[...] = jnp.zeros_like(carry_sc)

    slot = lax.rem(g, 2)
    y_ref[...] = _ffn_stage(x1_sc[1 - slot], w_up_ref, w_down_ref, nffn_ref, ck_ref, cb_ref, nfin_ref,
                            act_sc, carry_sc, nb=1, t=t)
    x1_sc[slot] = _mixer_stage(x_ref, cos_ref, sin_ref, w_in_ref, w_out_ref, nmix_ref, retg_ref, hgg_ref,
                               lbl_ref, proj_sc, merged_sc, oh_sc, qs_sc, ks_sc, sret_sc, shgt_sc,
                               nb=1, t=t)

    @pl.when(tm == nt - 1)
    def _():
        _store_states(sret_ref, shg_ref, sret_sc, shgt_sc, 1)

    @pl.when(jnp.logical_and(tf == nt - 1, g > 0))
    def _():
        cache_out_ref[...] = carry_sc[...]


def _const_spec(shape, ngrid):
    if ngrid == 1:
        imap = lambda g: (0,) * len(shape)
    else:
        imap = lambda bi, ti: (0,) * len(shape)
    return pl.BlockSpec(shape, imap, pipeline_mode=pl.Buffered(1))


def _prompt_pass(x2d, cos2, sin2, sret0, shg0, cache, weights, *, nseq, seqlen, t):
    (w_in, w_out, w_up, w_down, nmix, retg, hgg, lbl, nffn, ck, cb, nfin) = weights
    nt = seqlen // t
    ntiles = nseq * nt
    mtile = lambda g: jnp.minimum(g, ntiles - 1)
    mseq = lambda g: jnp.minimum(g // nt, nseq - 1)
    ftile = lambda g: jnp.maximum(g - 1, 0)
    fseq = lambda g: jnp.maximum(g - 1, 0) // nt
    state_in = pl.BlockSpec((1, N_HEADS, D_HEAD, D_HEAD), lambda g: (mseq(g), 0, 0, 0))
    cspec = functools.partial(_const_spec, ngrid=1)
    return pl.pallas_call(
        functools.partial(_prompt_kernel, t=t, nt=nt),
        grid=(ntiles + 1,),
        in_specs=[
            pl.BlockSpec((t, D_MODEL), lambda g: (mtile(g), 0)),
            pl.BlockSpec((t, D_HEAD), lambda g: (mtile(g) % nt, 0)),
            pl.BlockSpec((t, D_HEAD), lambda g: (mtile(g) % nt, 0)),
            state_in,
            state_in,
            pl.BlockSpec((1, CONV_W - 1, D_FF), lambda g: (fseq(g), 0, 0)),
            cspec((D_MODEL, IN_COLS)),
            cspec((2 * GROUP_W, D_MODEL)),
            cspec((D_MODEL, 2 * D_FF)),
            cspec((D_FF, D_MODEL)),
            cspec((1, D_MODEL)),
            cspec((1, GROUP_W)),
            cspec((1, GROUP_W)),
            cspec((2, GROUP_W)),
            cspec((1, D_MODEL)),
            cspec((CONV_W, D_FF)),
            cspec((1, D_FF)),
            cspec((1, D_MODEL)),
        ],
        out_specs=[
            pl.BlockSpec((t, D_MODEL), lambda g: (ftile(g), 0)),
            pl.BlockSpec((1, N_HEADS, D_HEAD, D_HEAD), lambda g: (mseq(g), 0, 0, 0)),
            pl.BlockSpec((1, N_HEADS, D_HEAD, D_HEAD), lambda g: (mseq(g), 0, 0, 0)),
            pl.BlockSpec((1, CONV_W - 1, D_FF), lambda g: (fseq(g), 0, 0)),
        ],
        out_shape=[
            jax.ShapeDtypeStruct((nseq * seqlen, D_MODEL), _F32),
            jax.ShapeDtypeStruct((nseq, N_HEADS, D_HEAD, D_HEAD), _F32),
            jax.ShapeDtypeStruct((nseq, N_HEADS, D_HEAD, D_HEAD), _F32),
            jax.ShapeDtypeStruct((nseq, CONV_W - 1, D_FF), _F32),
        ],
        scratch_shapes=_mixer_scratch(1, t) + _ffn_scratch(1, t) + [
            pltpu.VMEM((2, t, D_MODEL), _F32),
        ],
        compiler_params=pltpu.CompilerParams(
            dimension_semantics=("arbitrary",),
            vmem_limit_bytes=_VMEM_LIMIT_BYTES),
        name="prompt_trunk",
    )(x2d, cos2, sin2, sret0, shg0, cache, w_in, w_out, w_up, w_down,
      nmix, retg, hgg, lbl, nffn, ck, cb, nfin)


def _mixer_kernel(x_ref, cos_ref, sin_ref, sret0_ref, shg0_ref, w_in_ref, w_out_ref,
                  nmix_ref, retg_ref, hgg_ref, lbl_ref,
                  x1_ref, sret_ref, shg_ref,
                  proj_sc, merged_sc, oh_sc, qs_sc, ks_sc, sret_sc, shgt_sc, *, nb, t):
    _load_states(sret0_ref, shg0_ref, sret_sc, shgt_sc, nb)
    x1_ref[...] = _mixer_stage(x_ref, cos_ref, sin_ref, w_in_ref, w_out_ref, nmix_ref, retg_ref, hgg_ref,
                               lbl_ref, proj_sc, merged_sc, oh_sc, qs_sc, ks_sc, sret_sc, shgt_sc,
                               nb=nb, t=t)
    _store_states(sret_ref, shg_ref, sret_sc, shgt_sc, nb)


def _ffn_kernel(x1_ref, cache_ref, w_up_ref, w_down_ref, nffn_ref, ck_ref, cb_ref, nfin_ref,
                y_ref, cache_out_ref, act_sc, carry_sc, *, nb, t):
    carry_sc[...] = cache_ref[...]
    y_ref[...] = _ffn_stage(x1_ref[...], w_up_ref, w_down_ref, nffn_ref, ck_ref, cb_ref, nfin_ref,
                            act_sc, carry_sc, nb=nb, t=t)
    cache_out_ref[...] = carry_sc[...]


def _sample_pass(x2d, cos2, sin2, sret0, shg0, cache, weights, *, nb, t):
    (w_in, w_out, w_up, w_down, nmix, retg, hgg, lbl, nffn, ck, cb, nfin) = weights
    r = nb * t
    full = functools.partial(_const_spec, ngrid=1)
    whole = lambda shape: pl.BlockSpec(shape, lambda g: (0,) * len(shape))
    state_shape = (nb, N_HEADS, D_HEAD, D_HEAD)
    cache_shape = (nb, CONV_W - 1, D_FF)
    params = pltpu.CompilerParams(dimension_semantics=("arbitrary",), vmem_limit_bytes=_VMEM_LIMIT_BYTES)
    x1, ret_new, hg_new = pl.pallas_call(
        functools.partial(_mixer_kernel, nb=nb, t=t),
        grid=(1,),
        in_specs=[full((r, D_MODEL)), full((t, D_HEAD)), full((t, D_HEAD)), full(state_shape),
                  full(state_shape), full((D_MODEL, IN_COLS)), full((2 * GROUP_W, D_MODEL)),
                  full((1, D_MODEL)), full((1, GROUP_W)), full((1, GROUP_W)), full((2, GROUP_W))],
        out_specs=[whole((r, D_MODEL)), whole(state_shape), whole(state_shape)],
        out_shape=[jax.ShapeDtypeStruct((r, D_MODEL), _F32),
                   jax.ShapeDtypeStruct(state_shape, _F32),
                   jax.ShapeDtypeStruct(state_shape, _F32)],
        scratch_shapes=_mixer_scratch(nb, t),
        compiler_params=params,
        name="sample_mixer",
    )(x2d, cos2, sin2, sret0, shg0, w_in, w_out, nmix, retg, hgg, lbl)
    y, conv_new = pl.pallas_call(
        functools.partial(_ffn_kernel, nb=nb, t=t),
        grid=(1,),
        in_specs=[full((r, D_MODEL)), full(cache_shape), full((D_MODEL, 2 * D_FF)), full((D_FF, D_MODEL)),
                  full((1, D_MODEL)), full((CONV_W, D_FF)), full((1, D_FF)), full((1, D_MODEL))],
        out_specs=[whole((r, D_MODEL)), whole(cache_shape)],
        out_shape=[jax.ShapeDtypeStruct((r, D_MODEL), _F32),
                   jax.ShapeDtypeStruct(cache_shape, _F32)],
        scratch_shapes=_ffn_scratch(nb, t),
        compiler_params=params,
        name="sample_convffn",
    )(x1, cache, w_up, w_down, nffn, ck, cb, nfin)
    return y, ret_new, hg_new, conv_new


def _rope_tables(pos):
    half = D_HEAD // 2
    inv = 1.0 / (ROPE_BASE ** (jnp.arange(half, dtype=_F32) / half))
    ang = pos.astype(_F32)[:, None] * inv[None, :]
    cos, sin = jnp.cos(ang), jnp.sin(ang)
    return jnp.concatenate([cos, cos], axis=-1), jnp.concatenate([-sin, sin], axis=-1)


def kernel(x_prompt, x_sample, state_ret, state_hgrn, cache_ffn_conv, w_in, ret_norm_g, hg_norm_g,
           hg_lb_logits, w_out, norm_mix, norm_ffn, w_up, conv_k, conv_b, w_down, norm_final):
    assert w_in.shape[0] == 1, "single-layer trunk"
    weights = (
        w_in[0].astype(_BF16), w_out[0].astype(_BF16), w_up[0].astype(_BF16), w_down[0].astype(_BF16),
        norm_mix[0][None].astype(_F32), ret_norm_g[0][None].astype(_F32), hg_norm_g[0][None].astype(_F32),
        hg_lb_logits.astype(_F32), norm_ffn[0][None].astype(_F32), conv_k[0].astype(_F32),
        conv_b[0][None].astype(_F32), norm_final[None].astype(_F32),
    )
    nseq_p, seq_p, _ = x_prompt.shape
    nseq_s, seq_s, _ = x_sample.shape

    cos_p, sin_p = _rope_tables(jnp.arange(seq_p, dtype=jnp.int32))
    zeros_state = jnp.zeros((nseq_p, N_HEADS, D_HEAD, D_HEAD), _F32)
    zeros_conv = jnp.zeros((nseq_p, CONV_W - 1, D_FF), _F32)
    y_p, ret_p, hg_p, conv_p = _prompt_pass(
        x_prompt.reshape(nseq_p * seq_p, D_MODEL), cos_p, sin_p, zeros_state, zeros_state, zeros_conv,
        weights, nseq=nseq_p, seqlen=seq_p, t=_PROMPT_TILE)

    cos_s, sin_s = _rope_tables(PAST_LEN + jnp.arange(seq_s, dtype=jnp.int32))
    y_s, ret_s, hg_s, conv_s = _sample_pass(
        x_sample.reshape(nseq_s * seq_s, D_MODEL), cos_s, sin_s, state_ret[0].astype(_F32),
        state_hgrn[0].astype(_F32), cache_ffn_conv[0].astype(_F32), weights, nb=nseq_s, t=seq_s)

    return (y_p.reshape(nseq_p, seq_p, D_MODEL), y_s.reshape(nseq_s, seq_s, D_MODEL),
            ret_p[None], hg_p[None], conv_p[None], ret_s[None], hg_s[None], conv_s[None])
```

```python
import functools
import math

import jax
import jax.numpy as jnp
from jax import lax
from jax.experimental import pallas as pl
from jax.experimental.pallas import tpu as pltpu

D_MODEL = 1024
N_HEADS = 4
D_HEAD = 128
GROUP_W = N_HEADS * D_HEAD
IN_COLS = 8 * GROUP_W
D_FF = 2816
CONV_W = 3
EPS = 1e-6
ROPE_BASE = 10000.0
PAST_LEN = 1024

_RQ, _RK, _RV, _RG, _HQ, _HF, _HI, _HGATE = (i * GROUP_W for i in range(8))

_VMEM_LIMIT_BYTES = 58 * 1024 * 1024
_MXU_COLS = 256
_SUBLANES = 8
_PROMPT_TILE = 256

_F32 = jnp.float32
_BF16 = jnp.bfloat16


def _sigmoid(x):
    return 1.0 / (1.0 + jnp.exp(-x))


def _silu(x):
    return x * _sigmoid(x)


def _dot(a, b):
    return jnp.dot(a, b, preferred_element_type=_F32)


def _dot_nt(a, b):
    return lax.dot_general(a, b, (((1,), (1,)), ((), ())), preferred_element_type=_F32)


def _dot_tn(a, b):
    return lax.dot_general(a, b, (((0,), (0,)), ((), ())), preferred_element_type=_F32)


def _split3(x):
    x1 = x.astype(_BF16)
    r1 = x - x1.astype(_F32)
    x2 = r1.astype(_BF16)
    r2 = r1 - x2.astype(_F32)
    return x1, x2, r2.astype(_BF16)


def _levels(t):
    out = []
    s = t // 2
    while s >= 1:
        out.append(s)
        s //= 2
    return out


def _rmsnorm(x, g):
    return x * lax.rsqrt(jnp.mean(x * x, axis=-1, keepdims=True) + EPS) * g


def _interleave(a, b):
    out, ia, ib = [], 0, 0
    while ia < len(a) or ib < len(b):
        if ib >= len(b) or (ia < len(a) and ia * len(b) <= ib * len(a)):
            out.append(a[ia])
            ia += 1
        else:
            out.append(b[ib])
            ib += 1
    return out


def _ret_log_decay(hd):
    return math.log(1.0 - 2.0 ** (-5.0 - hd))


def _pair_constants(t):
    ii = lax.broadcasted_iota(jnp.int32, (t, t), 0)
    jj = lax.broadcasted_iota(jnp.int32, (t, t), 1)
    rel = ii - jj
    relf = jnp.maximum(rel, 0).astype(_F32)
    causal = rel >= 0
    tri = jnp.where(causal, 1.0, 0.0).astype(_BF16)
    xor_f = jnp.bitwise_xor(ii, jj).astype(_F32)
    pair_level = (pltpu.bitcast(xor_f, jnp.int32) >> 23) - 127
    pair_level = jnp.where(rel > 0, pair_level, jnp.where(rel == 0, -1, -2))
    ret_dmat = [jnp.where(causal, jnp.exp(_ret_log_decay(hd) * relf), 0.0) for hd in range(N_HEADS)]
    return tri, pair_level, ret_dmat


def _mixer_pieces(x_ref, cos_ref, sin_ref, w_in_ref, w_out_ref, nmix_ref, retg_ref, hgg_ref, lbl_ref,
                  proj_sc, merged_sc, oh_sc, qs_sc, ks_sc, sret_sc, shgt_sc, emit_x1, *, nb, t,
                  pair_consts=None):
    levels = _levels(t)
    c = {}

    def norm_in():
        c["hb"] = _rmsnorm(x_ref[...], nmix_ref[...]).astype(_BF16)

    def proj(cg):
        cols = slice(cg * GROUP_W, (cg + 1) * GROUP_W)
        proj_sc[:, cols] = _dot(c["hb"], w_in_ref[:, cols])

    def setup():
        l0 = lbl_ref[0:1, :]
        l1 = lbl_ref[1:2, :]
        lmax = jnp.maximum(l0, l1)
        e0 = jnp.exp(l0 - lmax)
        e1 = jnp.exp(l1 - lmax)
        c["lb"] = e0 / (e0 + e1)
        c["cos"] = cos_ref[...]
        c["sin"] = sin_ref[...]
        c["tpos"] = lax.broadcasted_iota(jnp.int32, (t, 1), 0)
        if pair_consts is None:
            c["tri"], c["pair_level"], c["ret_dmat"] = _pair_constants(t)
        else:
            tri_ref, level_ref, dmat_ref = pair_consts
            c["tri"] = tri_ref[...]
            c["pair_level"] = level_ref[...]
            c["ret_dmat"] = [dmat_ref[hd] for hd in range(N_HEADS)]

    def rows_of(i):
        r0 = i * t if isinstance(i, int) else pl.multiple_of(i * t, t)
        return pl.ds(r0, t)

    def ret_head(i, hd):
        rows = rows_of(i)
        log_g = _ret_log_decay(hd)
        tposf = c["tpos"].astype(_F32)
        hc = slice(hd * D_HEAD, (hd + 1) * D_HEAD)
        q = proj_sc[rows, _RQ + hd * D_HEAD:_RQ + (hd + 1) * D_HEAD]
        k = proj_sc[rows, _RK + hd * D_HEAD:_RK + (hd + 1) * D_HEAD]
        v = proj_sc[rows, _RV + hd * D_HEAD:_RV + (hd + 1) * D_HEAD]
        gate = proj_sc[rows, _RG + hd * D_HEAD:_RG + (hd + 1) * D_HEAD]
        q = q * c["cos"] + pltpu.roll(q, D_HEAD // 2, 1) * c["sin"]
        k = (k * c["cos"] + pltpu.roll(k, D_HEAD // 2, 1) * c["sin"]) * (D_HEAD ** -0.5)
        vb = v.astype(_BF16)
        attn = _dot_nt(q.astype(_BF16), k.astype(_BF16)) * c["ret_dmat"][hd]
        s_old = sret_sc[i, hd]
        q_dec = jnp.exp(log_g * (tposf + 1.0))
        k_dec = jnp.exp(log_g * (float(t - 1) - tposf))
        o = _dot(attn.astype(_BF16), vb) + _dot((q * q_dec).astype(_BF16), s_old.astype(_BF16))
        sret_sc[i, hd] = s_old * math.exp(log_g * t) + _dot_tn((k * k_dec).astype(_BF16), vb)
        o = o * lax.rsqrt(jnp.mean(o * o, axis=-1, keepdims=True) + EPS)
        o = o * retg_ref[:, hc] * _silu(gate)
        merged_sc[rows, hc] = o.astype(_BF16)

    def hg_gates(i, d):
        rows = rows_of(i)
        lb = c["lb"]
        f = lb + (1.0 - lb) * _sigmoid(proj_sc[rows, _HF:_HF + GROUP_W])
        g = jnp.log(f)
        d["g"] = g
        d["kk"] = 1.0 - f
        d["hq"] = proj_sc[rows, _HQ:_HQ + GROUP_W]
        g1, g2, g3 = _split3(g)
        tri = c["tri"]
        d["b"] = _dot(tri, g1) + _dot(tri, g2) + _dot(tri, g3)

    def hg_levels(i, d):
        rows = rows_of(i)
        g, b, hq, kk, tpos = d["g"], d["b"], d["hq"], d["kk"], c["tpos"]
        qs_sc[0, rows, :] = hq.astype(_BF16)
        ks_sc[0, rows, :] = kk.astype(_BF16)
        for li, s in enumerate(levels):
            nblk = t // (2 * s)
            if s >= 8:
                b3 = b.reshape(nblk, 2 * s, GROUP_W)
                hq3 = hq.reshape(nblk, 2 * s, GROUP_W)
                kk3 = kk.reshape(nblk, 2 * s, GROUP_W)
                ref_row = b3[:, s - 1:s, :]
                k_top = kk3[:, :s, :] * jnp.exp(ref_row - b3[:, :s, :])
                q_bot = hq3[:, s:, :] * jnp.exp(b3[:, s:, :] - ref_row)
                zeros = jnp.zeros((nblk, s, GROUP_W), _F32)
                qs_sc[li + 1, rows, :] = jnp.concatenate(
                    [zeros, q_bot], axis=1).reshape(t, GROUP_W).astype(_BF16)
                ks_sc[li + 1, rows, :] = jnp.concatenate(
                    [k_top, zeros], axis=1).reshape(t, GROUP_W).astype(_BF16)
                continue
            if s == 4:
                b3 = b.reshape(nblk, 2 * s, GROUP_W)
                ref_row = b3[:, s - 1:s, :]
                e = jnp.exp(-jnp.abs(b3 - ref_row)).reshape(t, GROUP_W)
            elif s == 2:
                g_prev = pltpu.roll(g, 1, 0)
                g_next = pltpu.roll(g, t - 1, 0)
                m4 = jnp.bitwise_and(tpos, 3)
                dd = jnp.where(m4 == 0, g_next, jnp.where(m4 == 1, 0.0, jnp.where(m4 == 2, g, g + g_prev)))
                e = jnp.exp(dd)
            else:
                e = jnp.exp(jnp.where(jnp.bitwise_and(tpos, 1) == 1, g, 0.0))
            qs_sc[li + 1, rows, :] = (hq * e).astype(_BF16)
            ks_sc[li + 1, rows, :] = (kk * e).astype(_BF16)

    def hg_head(i, hd, d):
        rows = rows_of(i)
        pair_level = c["pair_level"]
        hc = slice(hd * D_HEAD, (hd + 1) * D_HEAD)
        a = jnp.where(pair_level == -1, _dot_nt(qs_sc[0, rows, hc], ks_sc[0, rows, hc]), 0.0)
        for li, s in enumerate(levels):
            p = _dot_nt(qs_sc[li + 1, rows, hc], ks_sc[li + 1, rows, hc])
            a = jnp.where(pair_level == int(math.log2(s)), p, a)
        bh = d["b"][:, hc]
        vb = proj_sc[rows, _HI + hd * D_HEAD:_HI + (hd + 1) * D_HEAD].astype(_BF16)
        st_old = shgt_sc[i, hd]
        o = _dot(a.astype(_BF16), vb) + _dot_nt((d["hq"][:, hc] * jnp.exp(bh)).astype(_BF16),
                                                st_old.astype(_BF16))
        b_last = bh[t - 1:t, :]
        ke = d["kk"][:, hc] * jnp.exp(b_last - bh)
        shgt_sc[i, hd] = st_old * jnp.exp(b_last) + _dot_tn(vb, ke.astype(_BF16))
        oh_sc[rows, hc] = o

    def hg_out(i):
        rows = rows_of(i)
        oh = _rmsnorm(oh_sc[rows, :], hgg_ref[...])
        oh = oh * _silu(proj_sc[rows, _HGATE:_HGATE + GROUP_W])
        merged_sc[rows, GROUP_W:2 * GROUP_W] = oh.astype(_BF16)

    def tail():
        emit_x1(x_ref[...] + _dot(merged_sc[...], w_out_ref[...]))

    proj_pieces = [norm_in] + [functools.partial(proj, cg) for cg in range(8)]
    if nb == 1:
        d = {}
        attn_pieces = [setup]
        attn_pieces += [functools.partial(ret_head, 0, hd) for hd in range(N_HEADS)]
        attn_pieces += [functools.partial(hg_gates, 0, d), functools.partial(hg_levels, 0, d)]
        attn_pieces += [functools.partial(hg_head, 0, hd, d) for hd in range(N_HEADS)]
        attn_pieces += [functools.partial(hg_out, 0)]
    else:
        def all_sequences():
            setup()

            def body(i, carry):
                d = {}
                for hd in range(N_HEADS):
                    ret_head(i, hd)
                hg_gates(i, d)
                hg_levels(i, d)
                for hd in range(N_HEADS):
                    hg_head(i, hd, d)
                hg_out(i)
                return carry
            lax.fori_loop(0, nb, body, 0)
        attn_pieces = [all_sequences]
    return proj_pieces, attn_pieces, [tail]


def _load_states(sret0_ref, shg0_ref, sret_sc, shgt_sc, nb):
    sret_sc[...] = sret0_ref[...]
    for i in range(nb):
        for hd in range(N_HEADS):
            shgt_sc[i, hd] = shg0_ref[i, hd].T


def _store_states(sret_ref, shg_ref, sret_sc, shgt_sc, nb):
    sret_ref[...] = sret_sc[...]
    for i in range(nb):
        for hd in range(N_HEADS):
            shg_ref[i, hd] = shgt_sc[i, hd].T


def _mixer_scratch(nb, t):
    r = nb * t
    nlev = len(_levels(t)) + 1
    return [
        pltpu.VMEM((r, IN_COLS), _F32),
        pltpu.VMEM((r, 2 * GROUP_W), _BF16),
        pltpu.VMEM((r, GROUP_W), _F32),
        pltpu.VMEM((nlev, r, GROUP_W), _BF16),
        pltpu.VMEM((nlev, r, GROUP_W), _BF16),
        pltpu.VMEM((nb, N_HEADS, D_HEAD, D_HEAD), _F32),
        pltpu.VMEM((nb, N_HEADS, D_HEAD, D_HEAD), _F32),
    ]


def _ff_chunks():
    step = 2 * _MXU_COLS
    return [(c0, min(c0 + step, D_FF)) for c0 in range(0, D_FF, step)]


def _ffn_pieces(get_x1, get_hb, w_up_ref, w_down_ref, ck_ref, cb_ref, nfin_ref, act_sc, carry_sc,
                emit_y, *, nb, t):
    r = nb * t
    c = {"x2": []}

    def up(c0, c1):
        w = c1 - c0
        cols = slice(c0, c1)
        hb = get_hb()
        a3 = _dot(hb, w_up_ref[:, cols]).reshape(nb, t, w)
        bgate = _dot(hb, w_up_ref[:, D_FF + c0:D_FF + c1])
        c_m2 = carry_sc[:, 0:1, cols]
        c_m1 = carry_sc[:, 1:2, cols]
        tpos = lax.broadcasted_iota(jnp.int32, (nb, _SUBLANES, 1), 1)
        r1 = pltpu.roll(a3, 1, 1)
        r2 = pltpu.roll(a3, 2, 1)
        head1 = jnp.where(tpos == 0, c_m1, r1[:, :_SUBLANES, :])
        head2 = jnp.where(tpos == 0, c_m2, jnp.where(tpos == 1, c_m1, r2[:, :_SUBLANES, :]))
        a_m1 = jnp.concatenate([head1, r1[:, _SUBLANES:, :]], axis=1)
        a_m2 = jnp.concatenate([head2, r2[:, _SUBLANES:, :]], axis=1)
        conv = cb_ref[:, cols] + a_m2 * ck_ref[0:1, cols] + a_m1 * ck_ref[1:2, cols] + a3 * ck_ref[2:3, cols]
        act_sc[:, cols] = (_silu(conv).reshape(r, w) * bgate).astype(_BF16)
        carry_sc[:, :, cols] = a3[:, t - 2:t, :]

    down_w = 2 * _MXU_COLS

    def down(c0):
        cols = slice(c0, c0 + down_w)
        c["x2"].append(get_x1(cols) + _dot(act_sc[...], w_down_ref[:, cols]))

    def tail():
        emit_y(_rmsnorm(jnp.concatenate(c["x2"], axis=-1), nfin_ref[...]))

    up_pieces = [functools.partial(up, c0, c1) for c0, c1 in _ff_chunks()]
    down_pieces = [functools.partial(down, c0) for c0 in range(0, D_MODEL, down_w)]
    return up_pieces, down_pieces, [tail]


def _ffn_scratch(nb, t):
    return [
        pltpu.VMEM((nb * t, D_FF), _BF16),
        pltpu.VMEM((nb, CONV_W - 1, D_FF), _F32),
    ]


def _prompt_kernel(x_ref, cos_ref, sin_ref, sret0_ref, shg0_ref, cache_ref,
                   w_in_ref, w_out_ref, w_up_ref, w_down_ref,
                   nmix_ref, retg_ref, hgg_ref, lbl_ref, nffn_ref, ck_ref, cb_ref, nfin_ref,
                   y_ref, sret_ref, shg_ref, cache_out_ref,
                   proj_sc, merged_sc, oh_sc, qs_sc, ks_sc, sret_sc, shgt_sc, act_sc, carry_sc,
                   x1_sc, hbf_sc, tri_sc, level_sc, dmat_sc, *, t, nt):
    g = pl.program_id(0)
    tm = lax.rem(g, nt)
    tf = lax.rem(g + (nt - 1), nt)

    @pl.when(tm == 0)
    def _():
        _load_states(sret0_ref, shg0_ref, sret_sc, shgt_sc, 1)

    @pl.when(tf == 0)
    def _():
        carry_sc[...] = cache_ref[...]

    @pl.when(g == 0)
    def _():
        x1_sc[...] = jnp.zeros_like(x1_sc)
        hbf_sc[...] = jnp.zeros_like(hbf_sc)
        carry_sc[...] = jnp.zeros_like(carry_sc)
        tri, pair_level, ret_dmat = _pair_constants(t)
        tri_sc[...] = tri
        level_sc[...] = pair_level
        for hd in range(N_HEADS):
            dmat_sc[hd] = ret_dmat[hd]

    slot = lax.rem(g, 2)

    def emit_x1(x1):
        x1_sc[slot] = x1
        hbf_sc[slot] = _rmsnorm(x1, nffn_ref[...]).astype(_BF16)

    def emit_y(y):
        y_ref[...] = y

    m_proj, m_attn, m_tail = _mixer_pieces(
        x_ref, cos_ref, sin_ref, w_in_ref, w_out_ref, nmix_ref, retg_ref, hgg_ref, lbl_ref,
        proj_sc, merged_sc, oh_sc, qs_sc, ks_sc, sret_sc, shgt_sc, emit_x1, nb=1, t=t,
        pair_consts=(tri_sc, level_sc, dmat_sc))
    f_up, f_down, f_tail = _ffn_pieces(
        lambda cols: x1_sc[1 - slot, :, cols], lambda: hbf_sc[1 - slot], w_up_ref, w_down_ref, ck_ref, cb_ref,
        nfin_ref, act_sc, carry_sc, emit_y, nb=1, t=t)
    order = m_proj + _interleave(m_attn + m_tail, f_up + f_down[:1]) + f_down[1:] + f_tail
    for piece in order:
        piece()

    @pl.when(tm == nt - 1)
    def _():
        _store_states(sret_ref, shg_ref, sret_sc, shgt_sc, 1)

    @pl.when(jnp.logical_and(tf == nt - 1, g > 0))
    def _():
        cache_out_ref[...] = carry_sc[...]


def _const_spec(shape, ngrid):
    if ngrid == 1:
        imap = lambda g: (0,) * len(shape)
    else:
        imap = lambda bi, ti: (0,) * len(shape)
    return pl.BlockSpec(shape, imap, pipeline_mode=pl.Buffered(1))


def _prompt_pass(x2d, cos2, sin2, sret0, shg0, cache, weights, *, nseq, seqlen, t):
    (w_in, w_out, w_up, w_down, nmix, retg, hgg, lbl, nffn, ck, cb, nfin) = weights
    nt = seqlen // t
    ntiles = nseq * nt
    mtile = lambda g: jnp.minimum(g, ntiles - 1)
    mseq = lambda g: jnp.minimum(g // nt, nseq - 1)
    ftile = lambda g: jnp.maximum(g - 1, 0)
    fseq = lambda g: jnp.maximum(g - 1, 0) // nt
    state_in = pl.BlockSpec((1, N_HEADS, D_HEAD, D_HEAD), lambda g: (mseq(g), 0, 0, 0))
    cspec = functools.partial(_const_spec, ngrid=1)
    return pl.pallas_call(
        functools.partial(_prompt_kernel, t=t, nt=nt),
        grid=(ntiles + 1,),
        in_specs=[
            pl.BlockSpec((t, D_MODEL), lambda g: (mtile(g), 0)),
            pl.BlockSpec((t, D_HEAD), lambda g: (mtile(g) % nt, 0)),
            pl.BlockSpec((t, D_HEAD), lambda g: (mtile(g) % nt, 0)),
            state_in,
            state_in,
            pl.BlockSpec((1, CONV_W - 1, D_FF), lambda g: (fseq(g), 0, 0)),
            cspec((D_MODEL, IN_COLS)),
            cspec((2 * GROUP_W, D_MODEL)),
            cspec((D_MODEL, 2 * D_FF)),
            cspec((D_FF, D_MODEL)),
            cspec((1, D_MODEL)),
            cspec((1, GROUP_W)),
            cspec((1, GROUP_W)),
            cspec((2, GROUP_W)),
            cspec((1, D_MODEL)),
            cspec((CONV_W, D_FF)),
            cspec((1, D_FF)),
            cspec((1, D_MODEL)),
        ],
        out_specs=[
            pl.BlockSpec((t, D_MODEL), lambda g: (ftile(g), 0)),
            pl.BlockSpec((1, N_HEADS, D_HEAD, D_HEAD), lambda g: (mseq(g), 0, 0, 0)),
            pl.BlockSpec((1, N_HEADS, D_HEAD, D_HEAD), lambda g: (mseq(g), 0, 0, 0)),
            pl.BlockSpec((1, CONV_W - 1, D_FF), lambda g: (fseq(g), 0, 0)),
        ],
        out_shape=[
            jax.ShapeDtypeStruct((nseq * seqlen, D_MODEL), _F32),
            jax.ShapeDtypeStruct((nseq, N_HEADS, D_HEAD, D_HEAD), _F32),
            jax.ShapeDtypeStruct((nseq, N_HEADS, D_HEAD, D_HEAD), _F32),
            jax.ShapeDtypeStruct((nseq, CONV_W - 1, D_FF), _F32),
        ],
        scratch_shapes=_mixer_scratch(1, t) + _ffn_scratch(1, t) + [
            pltpu.VMEM((2, t, D_MODEL), _F32),
            pltpu.VMEM((2, t, D_MODEL), _BF16),
            pltpu.VMEM((t, t), _BF16),
            pltpu.VMEM((t, t), jnp.int32),
            pltpu.VMEM((N_HEADS, t, t), _F32),
        ],
        compiler_params=pltpu.CompilerParams(
            dimension_semantics=("arbitrary",),
            vmem_limit_bytes=_VMEM_LIMIT_BYTES),
        name="prompt_trunk",
    )(x2d, cos2, sin2, sret0, shg0, cache, w_in, w_out, w_up, w_down,
      nmix, retg, hgg, lbl, nffn, ck, cb, nfin)


def _mixer_kernel(x_ref, cos_ref, sin_ref, sret0_ref, shg0_ref, w_in_ref, w_out_ref,
                  nmix_ref, retg_ref, hgg_ref, lbl_ref,
                  x1_ref, sret_ref, shg_ref,
                  proj_sc, merged_sc, oh_sc, qs_sc, ks_sc, sret_sc, shgt_sc, *, nb, t):
    _load_states(sret0_ref, shg0_ref, sret_sc, shgt_sc, nb)

    def emit_x1(x1):
        x1_ref[...] = x1

    m_proj, m_attn, m_tail = _mixer_pieces(
        x_ref, cos_ref, sin_ref, w_in_ref, w_out_ref, nmix_ref, retg_ref, hgg_ref, lbl_ref,
        proj_sc, merged_sc, oh_sc, qs_sc, ks_sc, sret_sc, shgt_sc, emit_x1, nb=nb, t=t)
    for piece in m_proj + m_attn + m_tail:
        piece()
    _store_states(sret_ref, shg_ref, sret_sc, shgt_sc, nb)


def _ffn_kernel(x1_ref, cache_ref, w_up_ref, w_down_ref, nffn_ref, ck_ref, cb_ref, nfin_ref,
                y_ref, cache_out_ref, act_sc, carry_sc, *, nb, t):
    carry_sc[...] = cache_ref[...]
    hb = _rmsnorm(x1_ref[...], nffn_ref[...]).astype(_BF16)

    def emit_y(y):
        y_ref[...] = y

    f_up, f_down, f_tail = _ffn_pieces(
        lambda cols: x1_ref[:, cols], lambda: hb, w_up_ref, w_down_ref, ck_ref, cb_ref, nfin_ref,
        act_sc, carry_sc, emit_y, nb=nb, t=t)
    for piece in f_up + f_down + f_tail:
        piece()
    cache_out_ref[...] = carry_sc[...]


def _sample_pass(x2d, cos2, sin2, sret0, shg0, cache, weights, *, nb, t):
    (w_in, w_out, w_up, w_down, nmix, retg, hgg, lbl, nffn, ck, cb, nfin) = weights
    r = nb * t
    full = functools.partial(_const_spec, ngrid=1)
    whole = lambda shape: pl.BlockSpec(shape, lambda g: (0,) * len(shape))
    state_shape = (nb, N_HEADS, D_HEAD, D_HEAD)
    cache_shape = (nb, CONV_W - 1, D_FF)
    params = pltpu.CompilerParams(dimension_semantics=("arbitrary",), vmem_limit_bytes=_VMEM_LIMIT_BYTES)
    x1, ret_new, hg_new = pl.pallas_call(
        functools.partial(_mixer_kernel, nb=nb, t=t),
        grid=(1,),
        in_specs=[full((r, D_MODEL)), full((t, D_HEAD)), full((t, D_HEAD)), full(state_shape),
                  full(state_shape), full((D_MODEL, IN_COLS)), full((2 * GROUP_W, D_MODEL)),
                  full((1, D_MODEL)), full((1, GROUP_W)), full((1, GROUP_W)), full((2, GROUP_W))],
        out_specs=[whole((r, D_MODEL)), whole(state_shape), whole(state_shape)],
        out_shape=[jax.ShapeDtypeStruct((r, D_MODEL), _F32),
                   jax.ShapeDtypeStruct(state_shape, _F32),
                   jax.ShapeDtypeStruct(state_shape, _F32)],
        scratch_shapes=_mixer_scratch(nb, t),
        compiler_params=params,
        name="sample_mixer",
    )(x2d, cos2, sin2, sret0, shg0, w_in, w_out, nmix, retg, hgg, lbl)
    y, conv_new = pl.pallas_call(
        functools.partial(_ffn_kernel, nb=nb, t=t),
        grid=(1,),
        in_specs=[full((r, D_MODEL)), full(cache_shape), full((D_MODEL, 2 * D_FF)), full((D_FF, D_MODEL)),
                  full((1, D_MODEL)), full((CONV_W, D_FF)), full((1, D_FF)), full((1, D_MODEL))],
        out_specs=[whole((r, D_MODEL)), whole(cache_shape)],
        out_shape=[jax.ShapeDtypeStruct((r, D_MODEL), _F32),
                   jax.ShapeDtypeStruct(cache_shape, _F32)],
        scratch_shapes=_ffn_scratch(nb, t),
        compiler_params=params,
        name="sample_convffn",
    )(x1, cache, w_up, w_down, nffn, ck, cb, nfin)
    return y, ret_new, hg_new, conv_new


def _rope_tables(pos):
    half = D_HEAD // 2
    inv = 1.0 / (ROPE_BASE ** (jnp.arange(half, dtype=_F32) / half))
    ang = pos.astype(_F32)[:, None] * inv[None, :]
    cos, sin = jnp.cos(ang), jnp.sin(ang)
    return jnp.concatenate([cos, cos], axis=-1), jnp.concatenate([-sin, sin], axis=-1)


def kernel(x_prompt, x_sample, state_ret, state_hgrn, cache_ffn_conv, w_in, ret_norm_g, hg_norm_g,
           hg_lb_logits, w_out, norm_mix, norm_ffn, w_up, conv_k, conv_b, w_down, norm_final):
    assert w_in.shape[0] == 1, "single-layer trunk"
    weights = (
        w_in[0].astype(_BF16), w_out[0].astype(_BF16), w_up[0].astype(_BF16), w_down[0].astype(_BF16),
        norm_mix[0][None].astype(_F32), ret_norm_g[0][None].astype(_F32), hg_norm_g[0][None].astype(_F32),
        hg_lb_logits.astype(_F32), norm_ffn[0][None].astype(_F32), conv_k[0].astype(_F32),
        conv_b[0][None].astype(_F32), norm_final[None].astype(_F32),
    )
    nseq_p, seq_p, _ = x_prompt.shape
    nseq_s, seq_s, _ = x_sample.shape

    cos_p, sin_p = _rope_tables(jnp.arange(seq_p, dtype=jnp.int32))
    zeros_state = jnp.zeros((nseq_p, N_HEADS, D_HEAD, D_HEAD), _F32)
    zeros_conv = jnp.zeros((nseq_p, CONV_W - 1, D_FF), _F32)
    y_p, ret_p, hg_p, conv_p = _prompt_pass(
        x_prompt.reshape(nseq_p * seq_p, D_MODEL), cos_p, sin_p, zeros_state, zeros_state, zeros_conv,
        weights, nseq=nseq_p, seqlen=seq_p, t=_PROMPT_TILE)

    cos_s, sin_s = _rope_tables(PAST_LEN + jnp.arange(seq_s, dtype=jnp.int32))
    y_s, ret_s, hg_s, conv_s = _sample_pass(
        x_sample.reshape(nseq_s * seq_s, D_MODEL), cos_s, sin_s, state_ret[0].astype(_F32),
        state_hgrn[0].astype(_F32), cache_ffn_conv[0].astype(_F32), weights, nb=nseq_s, t=seq_s)

    return (y_p.reshape(nseq_p, seq_p, D_MODEL), y_s.reshape(nseq_s, seq_s, D_MODEL),
            ret_p[None], hg_p[None], conv_p[None], ret_s[None], hg_s[None], conv_s[None])
```

```python
import functools
import math

import jax
import jax.numpy as jnp
from jax import lax
from jax.experimental import pallas as pl
from jax.experimental.pallas import tpu as pltpu

D_MODEL = 1024
N_HEADS = 4
D_HEAD = 128
GROUP_W = N_HEADS * D_HEAD
IN_COLS = 8 * GROUP_W
D_FF = 2816
CONV_W = 3
EPS = 1e-6
ROPE_BASE = 10000.0
PAST_LEN = 1024

_RQ, _RK, _RV, _RG, _HQ, _HF, _HI, _HGATE = (i * GROUP_W for i in range(8))

_VMEM_LIMIT_BYTES = 58 * 1024 * 1024
_MXU_COLS = 256
_SUBLANES = 8
_PROMPT_TILE = 256

_F32 = jnp.float32
_BF16 = jnp.bfloat16


def _sigmoid(x):
    return 1.0 / (1.0 + jnp.exp(-x))


def _silu(x):
    return x * _sigmoid(x)


def _dot(a, b):
    return jnp.dot(a, b, preferred_element_type=_F32)


def _dot_nt(a, b):
    return lax.dot_general(a, b, (((1,), (1,)), ((), ())), preferred_element_type=_F32)


def _dot_tn(a, b):
    return lax.dot_general(a, b, (((0,), (0,)), ((), ())), preferred_element_type=_F32)


def _split3(x):
    x1 = x.astype(_BF16)
    r1 = x - x1.astype(_F32)
    x2 = r1.astype(_BF16)
    r2 = r1 - x2.astype(_F32)
    return x1, x2, r2.astype(_BF16)


def _levels(t):
    out = []
    s = t // 2
    while s >= 1:
        out.append(s)
        s //= 2
    return out


def _rmsnorm(x, g):
    return x * lax.rsqrt(jnp.mean(x * x, axis=-1, keepdims=True) + EPS) * g


def _interleave(a, b):
    out, ia, ib = [], 0, 0
    while ia < len(a) or ib < len(b):
        if ib >= len(b) or (ia < len(a) and ia * len(b) <= ib * len(a)):
            out.append(a[ia])
            ia += 1
        else:
            out.append(b[ib])
            ib += 1
    return out


def _ret_log_decay(hd):
    return math.log(1.0 - 2.0 ** (-5.0 - hd))


def _pair_constants(t):
    ii = lax.broadcasted_iota(jnp.int32, (t, t), 0)
    jj = lax.broadcasted_iota(jnp.int32, (t, t), 1)
    rel = ii - jj
    relf = jnp.maximum(rel, 0).astype(_F32)
    causal = rel >= 0
    tri = jnp.where(causal, 1.0, 0.0).astype(_BF16)
    xor_f = jnp.bitwise_xor(ii, jj).astype(_F32)
    pair_level = (pltpu.bitcast(xor_f, jnp.int32) >> 23) - 127
    pair_level = jnp.where(rel > 0, pair_level, jnp.where(rel == 0, -1, -2)).astype(_F32).astype(_BF16)
    ret_dmat = [jnp.where(causal, jnp.exp(_ret_log_decay(hd) * relf), 0.0) for hd in range(N_HEADS)]
    return tri, pair_level, ret_dmat


def _mixer_pieces(x_ref, rope, w_in_ref, w_out_ref, nmix_ref, retg_ref, hgg_ref, lbl_ref,
                  proj_sc, merged_sc, oh_sc, qs_sc, ks_sc, sret_sc, shgt_sc, emit_x1, *, nb, t,
                  pair_consts=None):
    levels = _levels(t)
    c = {}

    def norm_in():
        c["hb"] = _rmsnorm(x_ref[...], nmix_ref[...]).astype(_BF16)

    def proj(cg):
        cols = slice(cg * GROUP_W, (cg + 1) * GROUP_W)
        proj_sc[:, cols] = _dot(c["hb"], w_in_ref[:, cols])

    def setup():
        l0 = lbl_ref[0:1, :]
        l1 = lbl_ref[1:2, :]
        lmax = jnp.maximum(l0, l1)
        e0 = jnp.exp(l0 - lmax)
        e1 = jnp.exp(l1 - lmax)
        c["lb"] = e0 / (e0 + e1)
        c["cos"], c["sin"] = rope()
        c["tpos"] = lax.broadcasted_iota(jnp.int32, (t, 1), 0)
        if pair_consts is None:
            c["tri"], c["pair_level"], c["ret_dmat"] = _pair_constants(t)
        else:
            tri_ref, level_ref, dmat_ref = pair_consts
            c["tri"] = tri_ref[...]
            c["pair_level"] = level_ref[...]
            c["ret_dmat"] = [dmat_ref[hd] for hd in range(N_HEADS)]

    def rows_of(i):
        r0 = i * t if isinstance(i, int) else pl.multiple_of(i * t, t)
        return pl.ds(r0, t)

    def ret_head(i, hd):
        rows = rows_of(i)
        log_g = _ret_log_decay(hd)
        tposf = c["tpos"].astype(_F32)
        hc = slice(hd * D_HEAD, (hd + 1) * D_HEAD)
        q = proj_sc[rows, _RQ + hd * D_HEAD:_RQ + (hd + 1) * D_HEAD]
        k = proj_sc[rows, _RK + hd * D_HEAD:_RK + (hd + 1) * D_HEAD]
        v = proj_sc[rows, _RV + hd * D_HEAD:_RV + (hd + 1) * D_HEAD]
        gate = proj_sc[rows, _RG + hd * D_HEAD:_RG + (hd + 1) * D_HEAD]
        q = q * c["cos"] + pltpu.roll(q, D_HEAD // 2, 1) * c["sin"]
        k = (k * c["cos"] + pltpu.roll(k, D_HEAD // 2, 1) * c["sin"]) * (D_HEAD ** -0.5)
        vb = v.astype(_BF16)
        attn = _dot_nt(q.astype(_BF16), k.astype(_BF16)) * c["ret_dmat"][hd]
        s_old = sret_sc[i, hd]
        q_dec = jnp.exp(log_g * (tposf + 1.0))
        k_dec = jnp.exp(log_g * (float(t - 1) - tposf))
        o = _dot(attn.astype(_BF16), vb) + _dot((q * q_dec).astype(_BF16), s_old.astype(_BF16))
        sret_sc[i, hd] = s_old * math.exp(log_g * t) + _dot_tn((k * k_dec).astype(_BF16), vb)
        o = o * lax.rsqrt(jnp.mean(o * o, axis=-1, keepdims=True) + EPS)
        o = o * retg_ref[:, hc] * _silu(gate)
        merged_sc[rows, hc] = o.astype(_BF16)

    def hg_gates(i, d):
        rows = rows_of(i)
        lb = c["lb"]
        f = lb + (1.0 - lb) * _sigmoid(proj_sc[rows, _HF:_HF + GROUP_W])
        g = jnp.log(f)
        d["g"] = g
        d["kk"] = 1.0 - f
        d["hq"] = proj_sc[rows, _HQ:_HQ + GROUP_W]
        g1, g2, g3 = _split3(g)
        tri = c["tri"]
        d["b"] = _dot(tri, g1) + _dot(tri, g2) + _dot(tri, g3)

    def hg_levels(i, d):
        rows = rows_of(i)
        g, b, hq, kk, tpos = d["g"], d["b"], d["hq"], d["kk"], c["tpos"]
        qs_sc[0, rows, :] = hq.astype(_BF16)
        ks_sc[0, rows, :] = kk.astype(_BF16)
        for li, s in enumerate(levels):
            nblk = t // (2 * s)
            if s >= 8:
                b3 = b.reshape(nblk, 2 * s, GROUP_W)
                hq3 = hq.reshape(nblk, 2 * s, GROUP_W)
                kk3 = kk.reshape(nblk, 2 * s, GROUP_W)
                ref_row = b3[:, s - 1:s, :]
                k_top = kk3[:, :s, :] * jnp.exp(ref_row - b3[:, :s, :])
                q_bot = hq3[:, s:, :] * jnp.exp(b3[:, s:, :] - ref_row)
                zeros = jnp.zeros((nblk, s, GROUP_W), _F32)
                qs_sc[li + 1, rows, :] = jnp.concatenate(
                    [zeros, q_bot], axis=1).reshape(t, GROUP_W).astype(_BF16)
                ks_sc[li + 1, rows, :] = jnp.concatenate(
                    [k_top, zeros], axis=1).reshape(t, GROUP_W).astype(_BF16)
                continue
            if s == 4:
                b3 = b.reshape(nblk, 2 * s, GROUP_W)
                ref_row = b3[:, s - 1:s, :]
                e = jnp.exp(-jnp.abs(b3 - ref_row)).reshape(t, GROUP_W)
            elif s == 2:
                g_prev = pltpu.roll(g, 1, 0)
                g_next = pltpu.roll(g, t - 1, 0)
                m4 = jnp.bitwise_and(tpos, 3)
                dd = jnp.where(m4 == 0, g_next, jnp.where(m4 == 1, 0.0, jnp.where(m4 == 2, g, g + g_prev)))
                e = jnp.exp(dd)
            else:
                e = jnp.exp(jnp.where(jnp.bitwise_and(tpos, 1) == 1, g, 0.0))
            qs_sc[li + 1, rows, :] = (hq * e).astype(_BF16)
            ks_sc[li + 1, rows, :] = (kk * e).astype(_BF16)

    def hg_head(i, hd, d):
        rows = rows_of(i)
        pair_level = c["pair_level"]
        hc = slice(hd * D_HEAD, (hd + 1) * D_HEAD)
        p = _dot_nt(qs_sc[0, rows, hc], ks_sc[0, rows, hc]).astype(_BF16)
        a = jnp.where(pair_level == -1.0, p, jnp.zeros_like(p))
        for li, s in enumerate(levels):
            p = _dot_nt(qs_sc[li + 1, rows, hc], ks_sc[li + 1, rows, hc]).astype(_BF16)
            a = jnp.where(pair_level == float(int(math.log2(s))), p, a)
        bh = d["b"][:, hc]
        vb = proj_sc[rows, _HI + hd * D_HEAD:_HI + (hd + 1) * D_HEAD].astype(_BF16)
        st_old = shgt_sc[i, hd]
        o = _dot(a, vb) + _dot_nt((d["hq"][:, hc] * jnp.exp(bh)).astype(_BF16), st_old.astype(_BF16))
        b_last = bh[t - 1:t, :]
        ke = d["kk"][:, hc] * jnp.exp(b_last - bh)
        shgt_sc[i, hd] = st_old * jnp.exp(b_last) + _dot_tn(vb, ke.astype(_BF16))
        oh_sc[rows, hc] = o

    def hg_out(i):
        rows = rows_of(i)
        oh = _rmsnorm(oh_sc[rows, :], hgg_ref[...])
        oh = oh * _silu(proj_sc[rows, _HGATE:_HGATE + GROUP_W])
        merged_sc[rows, GROUP_W:2 * GROUP_W] = oh.astype(_BF16)

    def tail():
        emit_x1(x_ref[...] + _dot(merged_sc[...], w_out_ref[...]))

    proj_pieces = [norm_in] + [functools.partial(proj, cg) for cg in range(8)]
    if nb == 1:
        d = {}
        attn_pieces = [setup]
        attn_pieces += [functools.partial(ret_head, 0, hd) for hd in range(N_HEADS)]
        attn_pieces += [functools.partial(hg_gates, 0, d), functools.partial(hg_levels, 0, d)]
        attn_pieces += [functools.partial(hg_head, 0, hd, d) for hd in range(N_HEADS)]
        attn_pieces += [functools.partial(hg_out, 0)]
    else:
        def all_sequences():
            setup()

            def body(i, carry):
                d = {}
                for hd in range(N_HEADS):
                    ret_head(i, hd)
                hg_gates(i, d)
                hg_levels(i, d)
                for hd in range(N_HEADS):
                    hg_head(i, hd, d)
                hg_out(i)
                return carry
            lax.fori_loop(0, nb, body, 0)
        attn_pieces = [all_sequences]
    return proj_pieces, attn_pieces, [tail]


def _load_states(sret0_ref, shg0_ref, sret_sc, shgt_sc, nb):
    sret_sc[...] = sret0_ref[...]
    for i in range(nb):
        for hd in range(N_HEADS):
            shgt_sc[i, hd] = shg0_ref[i, hd].T


def _store_states(sret_ref, shg_ref, sret_sc, shgt_sc, nb):
    sret_ref[...] = sret_sc[...]
    for i in range(nb):
        for hd in range(N_HEADS):
            shg_ref[i, hd] = shgt_sc[i, hd].T


def _mixer_scratch(nb, t):
    r = nb * t
    nlev = len(_levels(t)) + 1
    return [
        pltpu.VMEM((r, IN_COLS), _F32),
        pltpu.VMEM((r, 2 * GROUP_W), _BF16),
        pltpu.VMEM((r, GROUP_W), _F32),
        pltpu.VMEM((nlev, r, GROUP_W), _BF16),
        pltpu.VMEM((nlev, r, GROUP_W), _BF16),
        pltpu.VMEM((nb, N_HEADS, D_HEAD, D_HEAD), _F32),
        pltpu.VMEM((nb, N_HEADS, D_HEAD, D_HEAD), _F32),
    ]


def _ff_chunks():
    step = 2 * _MXU_COLS
    return [(c0, min(c0 + step, D_FF)) for c0 in range(0, D_FF, step)]


def _ffn_pieces(get_x1, get_hb, w_up_ref, w_down_ref, ck_ref, cb_ref, nfin_ref, act_sc, carry_sc,
                emit_y, *, nb, t):
    r = nb * t
    c = {"x2": []}

    def up(c0, c1):
        w = c1 - c0
        cols = slice(c0, c1)
        hb = get_hb()
        a3 = _dot(hb, w_up_ref[:, cols]).reshape(nb, t, w)
        bgate = _dot(hb, w_up_ref[:, D_FF + c0:D_FF + c1])
        c_m2 = carry_sc[:, 0:1, cols]
        c_m1 = carry_sc[:, 1:2, cols]
        tpos = lax.broadcasted_iota(jnp.int32, (nb, _SUBLANES, 1), 1)
        r1 = pltpu.roll(a3, 1, 1)
        r2 = pltpu.roll(a3, 2, 1)
        head1 = jnp.where(tpos == 0, c_m1, r1[:, :_SUBLANES, :])
        head2 = jnp.where(tpos == 0, c_m2, jnp.where(tpos == 1, c_m1, r2[:, :_SUBLANES, :]))
        a_m1 = jnp.concatenate([head1, r1[:, _SUBLANES:, :]], axis=1)
        a_m2 = jnp.concatenate([head2, r2[:, _SUBLANES:, :]], axis=1)
        conv = cb_ref[:, cols] + a_m2 * ck_ref[0:1, cols] + a_m1 * ck_ref[1:2, cols] + a3 * ck_ref[2:3, cols]
        act_sc[:, cols] = (_silu(conv).reshape(r, w) * bgate).astype(_BF16)
        carry_sc[:, :, cols] = a3[:, t - 2:t, :]

    down_w = 2 * _MXU_COLS

    def down(c0):
        cols = slice(c0, c0 + down_w)
        c["x2"].append(get_x1(cols) + _dot(act_sc[...], w_down_ref[:, cols]))

    def tail():
        emit_y(_rmsnorm(jnp.concatenate(c["x2"], axis=-1), nfin_ref[...]))

    up_pieces = [functools.partial(up, c0, c1) for c0, c1 in _ff_chunks()]
    down_pieces = [functools.partial(down, c0) for c0 in range(0, D_MODEL, down_w)]
    return up_pieces, down_pieces, [tail]


def _ffn_scratch(nb, t):
    return [
        pltpu.VMEM((nb * t, D_FF), _BF16),
        pltpu.VMEM((nb, CONV_W - 1, D_FF), _F32),
    ]


def _prompt_kernel(x_ref, cos_t_ref, sin_t_ref, cos_r_ref, sin_r_ref, cos_rs_ref, sin_rs_ref,
                   w_in_ref, w_out_ref, w_up_ref, w_down_ref,
                   nmix_ref, retg_ref, hgg_ref, lbl_ref, nffn_ref, ck_ref, cb_ref, nfin_ref,
                   y_ref, sret_ref, shg_ref, cache_out_ref,
                   proj_sc, merged_sc, oh_sc, qs_sc, ks_sc, sret_sc, shgt_sc, act_sc, carry_sc,
                   x1_sc, hbf_sc, tri_sc, level_sc, dmat_sc, *, t, nt):
    g = pl.program_id(0)
    tm = lax.rem(g, nt)
    tf = lax.rem(g + (nt - 1), nt)

    @pl.when(tm == 0)
    def _():
        sret_sc[...] = jnp.zeros_like(sret_sc)
        shgt_sc[...] = jnp.zeros_like(shgt_sc)

    @pl.when(tf == 0)
    def _():
        carry_sc[...] = jnp.zeros_like(carry_sc)

    @pl.when(g == 0)
    def _():
        x1_sc[...] = jnp.zeros_like(x1_sc)
        hbf_sc[...] = jnp.zeros_like(hbf_sc)
        carry_sc[...] = jnp.zeros_like(carry_sc)
        tri, pair_level, ret_dmat = _pair_constants(t)
        tri_sc[...] = tri
        level_sc[...] = pair_level
        for hd in range(N_HEADS):
            dmat_sc[hd] = ret_dmat[hd]

    slot = lax.rem(g, 2)

    def rope():
        cos_t = cos_t_ref[pl.ds(tm, 1), :]
        sin_t = sin_t_ref[pl.ds(tm, 1), :]
        cos = cos_t * cos_r_ref[...] - sin_t * sin_r_ref[...]
        sin_signed = sin_t * cos_rs_ref[...] + cos_t * sin_rs_ref[...]
        return cos, sin_signed

    def emit_x1(x1):
        x1_sc[slot] = x1
        hbf_sc[slot] = _rmsnorm(x1, nffn_ref[...]).astype(_BF16)

    def emit_y(y):
        y_ref[...] = y

    m_proj, m_attn, m_tail = _mixer_pieces(
        x_ref, rope, w_in_ref, w_out_ref, nmix_ref, retg_ref, hgg_ref, lbl_ref,
        proj_sc, merged_sc, oh_sc, qs_sc, ks_sc, sret_sc, shgt_sc, emit_x1, nb=1, t=t,
        pair_consts=(tri_sc, level_sc, dmat_sc))
    f_up, f_down, f_tail = _ffn_pieces(
        lambda cols: x1_sc[1 - slot, :, cols], lambda: hbf_sc[1 - slot], w_up_ref, w_down_ref, ck_ref, cb_ref,
        nfin_ref, act_sc, carry_sc, emit_y, nb=1, t=t)
    order = m_proj + _interleave(m_attn + m_tail, f_up + f_down[:1]) + f_down[1:] + f_tail
    for piece in order:
        piece()

    @pl.when(tm == nt - 1)
    def _():
        _store_states(sret_ref, shg_ref, sret_sc, shgt_sc, 1)

    @pl.when(jnp.logical_and(tf == nt - 1, g > 0))
    def _():
        cache_out_ref[...] = carry_sc[...]


def _const_spec(shape, ngrid):
    if ngrid == 1:
        imap = lambda g: (0,) * len(shape)
    else:
        imap = lambda bi, ti: (0,) * len(shape)
    return pl.BlockSpec(shape, imap, pipeline_mode=pl.Buffered(1))


def _rope_sum_tables(seqlen, t):
    half = D_HEAD // 2
    inv = 1.0 / (ROPE_BASE ** (jnp.arange(half, dtype=_F32) / half))
    dup = lambda x: jnp.concatenate([x, x], axis=-1)
    ang_t = (jnp.arange(seqlen // t, dtype=_F32) * t)[:, None] * inv[None, :]
    ang_r = jnp.arange(t, dtype=_F32)[:, None] * inv[None, :]
    sign = jnp.concatenate([-jnp.ones((half,), _F32), jnp.ones((half,), _F32)])[None, :]
    cos_r, sin_r = dup(jnp.cos(ang_r)), dup(jnp.sin(ang_r))
    return dup(jnp.cos(ang_t)), dup(jnp.sin(ang_t)), cos_r, sin_r, sign * cos_r, sign * sin_r


def _prompt_pass(x2d, weights, *, nseq, seqlen, t):
    (w_in, w_out, w_up, w_down, nmix, retg, hgg, lbl, nffn, ck, cb, nfin) = weights
    nt = seqlen // t
    ntiles = nseq * nt
    mtile = lambda g: jnp.minimum(g, ntiles - 1)
    mseq = lambda g: jnp.minimum(g // nt, nseq - 1)
    ftile = lambda g: jnp.maximum(g - 1, 0)
    fseq = lambda g: jnp.maximum(g - 1, 0) // nt
    cspec = functools.partial(_const_spec, ngrid=1)
    rope_tables = _rope_sum_tables(seqlen, t)
    return pl.pallas_call(
        functools.partial(_prompt_kernel, t=t, nt=nt),
        grid=(ntiles + 1,),
        in_specs=[
            pl.BlockSpec((t, D_MODEL), lambda g: (mtile(g), 0)),
            cspec((nt, D_HEAD)),
            cspec((nt, D_HEAD)),
            cspec((t, D_HEAD)),
            cspec((t, D_HEAD)),
            cspec((t, D_HEAD)),
            cspec((t, D_HEAD)),
            cspec((D_MODEL, IN_COLS)),
            cspec((2 * GROUP_W, D_MODEL)),
            cspec((D_MODEL, 2 * D_FF)),
            cspec((D_FF, D_MODEL)),
            cspec((1, D_MODEL)),
            cspec((1, GROUP_W)),
            cspec((1, GROUP_W)),
            cspec((2, GROUP_W)),
            cspec((1, D_MODEL)),
            cspec((CONV_W, D_FF)),
            cspec((1, D_FF)),
            cspec((1, D_MODEL)),
        ],
        out_specs=[
            pl.BlockSpec((t, D_MODEL), lambda g: (ftile(g), 0)),
            pl.BlockSpec((1, N_HEADS, D_HEAD, D_HEAD), lambda g: (mseq(g), 0, 0, 0)),
            pl.BlockSpec((1, N_HEADS, D_HEAD, D_HEAD), lambda g: (mseq(g), 0, 0, 0)),
            pl.BlockSpec((1, CONV_W - 1, D_FF), lambda g: (fseq(g), 0, 0)),
        ],
        out_shape=[
            jax.ShapeDtypeStruct((nseq * seqlen, D_MODEL), _F32),
            jax.ShapeDtypeStruct((nseq, N_HEADS, D_HEAD, D_HEAD), _F32),
            jax.ShapeDtypeStruct((nseq, N_HEADS, D_HEAD, D_HEAD), _F32),
            jax.ShapeDtypeStruct((nseq, CONV_W - 1, D_FF), _F32),
        ],
        scratch_shapes=_mixer_scratch(1, t) + _ffn_scratch(1, t) + [
            pltpu.VMEM((2, t, D_MODEL), _F32),
            pltpu.VMEM((2, t, D_MODEL), _BF16),
            pltpu.VMEM((t, t), _BF16),
            pltpu.VMEM((t, t), _BF16),
            pltpu.VMEM((N_HEADS, t, t), _F32),
        ],
        compiler_params=pltpu.CompilerParams(
            dimension_semantics=("arbitrary",),
            vmem_limit_bytes=_VMEM_LIMIT_BYTES),
        name="prompt_trunk",
    )(x2d, *rope_tables, w_in, w_out, w_up, w_down,
      nmix, retg, hgg, lbl, nffn, ck, cb, nfin)


def _mixer_kernel(x_ref, cos_ref, sin_ref, sret0_ref, shg0_ref, w_in_ref, w_out_ref,
                  nmix_ref, retg_ref, hgg_ref, lbl_ref,
                  x1_ref, sret_ref, shg_ref,
                  proj_sc, merged_sc, oh_sc, qs_sc, ks_sc, sret_sc, shgt_sc, *, nb, t):
    _load_states(sret0_ref, shg0_ref, sret_sc, shgt_sc, nb)

    def emit_x1(x1):
        x1_ref[...] = x1

    m_proj, m_attn, m_tail = _mixer_pieces(
        x_ref, lambda: (cos_ref[...], sin_ref[...]), w_in_ref, w_out_ref, nmix_ref, retg_ref, hgg_ref,
        lbl_ref, proj_sc, merged_sc, oh_sc, qs_sc, ks_sc, sret_sc, shgt_sc, emit_x1, nb=nb, t=t)
    for piece in m_proj + m_attn + m_tail:
        piece()
    _store_states(sret_ref, shg_ref, sret_sc, shgt_sc, nb)


def _ffn_kernel(x1_ref, cache_ref, w_up_ref, w_down_ref, nffn_ref, ck_ref, cb_ref, nfin_ref,
                y_ref, cache_out_ref, act_sc, carry_sc, *, nb, t):
    carry_sc[...] = cache_ref[...]
    hb = _rmsnorm(x1_ref[...], nffn_ref[...]).astype(_BF16)

    def emit_y(y):
        y_ref[...] = y

    f_up, f_down, f_tail = _ffn_pieces(
        lambda cols: x1_ref[:, cols], lambda: hb, w_up_ref, w_down_ref, ck_ref, cb_ref, nfin_ref,
        act_sc, carry_sc, emit_y, nb=nb, t=t)
    for piece in f_up + f_down + f_tail:
        piece()
    cache_out_ref[...] = carry_sc[...]


def _sample_pass(x2d, cos2, sin2, sret0, shg0, cache, weights, *, nb, t):
    (w_in, w_out, w_up, w_down, nmix, retg, hgg, lbl, nffn, ck, cb, nfin) = weights
    r = nb * t
    full = functools.partial(_const_spec, ngrid=1)
    whole = lambda shape: pl.BlockSpec(shape, lambda g: (0,) * len(shape))
    state_shape = (nb, N_HEADS, D_HEAD, D_HEAD)
    cache_shape = (nb, CONV_W - 1, D_FF)
    params = pltpu.CompilerParams(dimension_semantics=("arbitrary",), vmem_limit_bytes=_VMEM_LIMIT_BYTES)
    x1, ret_new, hg_new = pl.pallas_call(
        functools.partial(_mixer_kernel, nb=nb, t=t),
        grid=(1,),
        in_specs=[full((r, D_MODEL)), full((t, D_HEAD)), full((t, D_HEAD)), full(state_shape),
                  full(state_shape), full((D_MODEL, IN_COLS)), full((2 * GROUP_W, D_MODEL)),
                  full((1, D_MODEL)), full((1, GROUP_W)), full((1, GROUP_W)), full((2, GROUP_W))],
        out_specs=[whole((r, D_MODEL)), whole(state_shape), whole(state_shape)],
        out_shape=[jax.ShapeDtypeStruct((r, D_MODEL), _F32),
                   jax.ShapeDtypeStruct(state_shape, _F32),
                   jax.ShapeDtypeStruct(state_shape, _F32)],
        scratch_shapes=_mixer_scratch(nb, t),
        compiler_params=params,
        name="sample_mixer",
    )(x2d, cos2, sin2, sret0, shg0, w_in, w_out, nmix, retg, hgg, lbl)
    y, conv_new = pl.pallas_call(
        functools.partial(_ffn_kernel, nb=nb, t=t),
        grid=(1,),
        in_specs=[full((r, D_MODEL)), full(cache_shape), full((D_MODEL, 2 * D_FF)), full((D_FF, D_MODEL)),
                  full((1, D_MODEL)), full((CONV_W, D_FF)), full((1, D_FF)), full((1, D_MODEL))],
        out_specs=[whole((r, D_MODEL)), whole(cache_shape)],
        out_shape=[jax.ShapeDtypeStruct((r, D_MODEL), _F32),
                   jax.ShapeDtypeStruct(cache_shape, _F32)],
        scratch_shapes=_ffn_scratch(nb, t),
        compiler_params=params,
        name="sample_convffn",
    )(x1, cache, w_up, w_down, nffn, ck, cb, nfin)
    return y, ret_new, hg_new, conv_new


def _rope_tables(pos):
    half = D_HEAD // 2
    inv = 1.0 / (ROPE_BASE ** (jnp.arange(half, dtype=_F32) / half))
    ang = pos.astype(_F32)[:, None] * inv[None, :]
    cos, sin = jnp.cos(ang), jnp.sin(ang)
    return jnp.concatenate([cos, cos], axis=-1), jnp.concatenate([-sin, sin], axis=-1)


def kernel(x_prompt, x_sample, state_ret, state_hgrn, cache_ffn_conv, w_in, ret_norm_g, hg_norm_g,
           hg_lb_logits, w_out, norm_mix, norm_ffn, w_up, conv_k, conv_b, w_down, norm_final):
    assert w_in.shape[0] == 1, "single-layer trunk"
    weights = (
        w_in[0].astype(_BF16), w_out[0].astype(_BF16), w_up[0].astype(_BF16), w_down[0].astype(_BF16),
        norm_mix[0][None].astype(_F32), ret_norm_g[0][None].astype(_F32), hg_norm_g[0][None].astype(_F32),
        hg_lb_logits.astype(_F32), norm_ffn[0][None].astype(_F32), conv_k[0].astype(_F32),
        conv_b[0][None].astype(_F32), norm_final[None].astype(_F32),
    )
    nseq_p, seq_p, _ = x_prompt.shape
    nseq_s, seq_s, _ = x_sample.shape

    y_p, ret_p, hg_p, conv_p = _prompt_pass(
        x_prompt.reshape(nseq_p * seq_p, D_MODEL), weights, nseq=nseq_p, seqlen=seq_p, t=_PROMPT_TILE)

    cos_s, sin_s = _rope_tables(PAST_LEN + jnp.arange(seq_s, dtype=jnp.int32))
    y_s, ret_s, hg_s, conv_s = _sample_pass(
        x_sample.reshape(nseq_s * seq_s, D_MODEL), cos_s, sin_s, state_ret[0].astype(_F32),
        state_hgrn[0].astype(_F32), cache_ffn_conv[0].astype(_F32), weights, nb=nseq_s, t=seq_s)

    return (y_p.reshape(nseq_p, seq_p, D_MODEL), y_s.reshape(nseq_s, seq_s, D_MODEL),
            ret_p[None], hg_p[None], conv_p[None], ret_s[None], hg_s[None], conv_s[None])
```

```python
import functools
import math

import jax
import jax.numpy as jnp
from jax import lax
from jax.experimental import pallas as pl
from jax.experimental.pallas import tpu as pltpu

D_MODEL = 1024
N_HEADS = 4
D_HEAD = 128
GROUP_W = N_HEADS * D_HEAD
IN_COLS = 8 * GROUP_W
D_FF = 2816
CONV_W = 3
EPS = 1e-6
ROPE_BASE = 10000.0
PAST_LEN = 1024

_RQ, _RK, _RV, _RG, _HQ, _HF, _HI, _HGATE = (i * GROUP_W for i in range(8))

_VMEM_LIMIT_BYTES = 58 * 1024 * 1024
_MXU_COLS = 256
_SUBLANES = 8
_PROMPT_TILE = 256

_F32 = jnp.float32
_BF16 = jnp.bfloat16


def _sigmoid(x):
    return 1.0 / (1.0 + jnp.exp(-x))


def _silu(x):
    return x * _sigmoid(x)


def _dot(a, b):
    return jnp.dot(a, b, preferred_element_type=_F32)


def _dot_nt(a, b):
    return lax.dot_general(a, b, (((1,), (1,)), ((), ())), preferred_element_type=_F32)


def _dot_tn(a, b):
    return lax.dot_general(a, b, (((0,), (0,)), ((), ())), preferred_element_type=_F32)


def _split3(x):
    x1 = x.astype(_BF16)
    r1 = x - x1.astype(_F32)
    x2 = r1.astype(_BF16)
    r2 = r1 - x2.astype(_F32)
    return x1, x2, r2.astype(_BF16)


def _levels(t):
    out = []
    s = t // 2
    while s >= 1:
        out.append(s)
        s //= 2
    return out


def _rmsnorm(x, g):
    return x * lax.rsqrt(jnp.mean(x * x, axis=-1, keepdims=True) + EPS) * g


def _interleave(a, b):
    out, ia, ib = [], 0, 0
    while ia < len(a) or ib < len(b):
        if ib >= len(b) or (ia < len(a) and ia * len(b) <= ib * len(a)):
            out.append(a[ia])
            ia += 1
        else:
            out.append(b[ib])
            ib += 1
    return out


def _ret_log_decay(hd):
    return math.log(1.0 - 2.0 ** (-5.0 - hd))


def _pair_constants(t):
    ii = lax.broadcasted_iota(jnp.int32, (t, t), 0)
    jj = lax.broadcasted_iota(jnp.int32, (t, t), 1)
    rel = ii - jj
    relf = jnp.maximum(rel, 0).astype(_F32)
    causal = rel >= 0
    tri = jnp.where(causal, 1.0, 0.0).astype(_BF16)
    xor_f = jnp.bitwise_xor(ii, jj).astype(_F32)
    pair_level = (pltpu.bitcast(xor_f, jnp.int32) >> 23) - 127
    pair_level = jnp.where(rel > 0, pair_level, jnp.where(rel == 0, -1, -2)).astype(_F32).astype(_BF16)
    ret_dmat = [jnp.where(causal, jnp.exp(_ret_log_decay(hd) * relf), 0.0) for hd in range(N_HEADS)]
    return tri, pair_level, ret_dmat


def _mixer_pieces(x_ref, rope, w_in_ref, w_out_ref, nmix_ref, retg_ref, hgg_ref, lbl_ref,
                  proj_sc, merged_sc, oh_sc, qs_sc, ks_sc, sret_sc, shgt_sc, emit_x1, *, nb, t,
                  pair_consts=None):
    levels = _levels(t)
    c = {}

    def norm_in():
        c["hb"] = _rmsnorm(x_ref[...], nmix_ref[...]).astype(_BF16)

    def proj(cg):
        cols = slice(cg * GROUP_W, (cg + 1) * GROUP_W)
        proj_sc[:, cols] = _dot(c["hb"], w_in_ref[:, cols])

    def setup():
        l0 = lbl_ref[0:1, :]
        l1 = lbl_ref[1:2, :]
        lmax = jnp.maximum(l0, l1)
        e0 = jnp.exp(l0 - lmax)
        e1 = jnp.exp(l1 - lmax)
        c["lb"] = e0 / (e0 + e1)
        c["cos"], c["sin"] = rope()
        c["tpos"] = lax.broadcasted_iota(jnp.int32, (t, 1), 0)
        if pair_consts is None:
            c["tri"], c["pair_level"], c["ret_dmat"] = _pair_constants(t)
        else:
            tri_ref, level_ref, dmat_ref = pair_consts
            c["tri"] = tri_ref[...]
            c["pair_level"] = level_ref[...]
            c["ret_dmat"] = [dmat_ref[hd] for hd in range(N_HEADS)]

    def rows_of(i):
        r0 = i * t if isinstance(i, int) else pl.multiple_of(i * t, t)
        return pl.ds(r0, t)

    def ret_head(i, hd):
        rows = rows_of(i)
        log_g = _ret_log_decay(hd)
        tposf = c["tpos"].astype(_F32)
        hc = slice(hd * D_HEAD, (hd + 1) * D_HEAD)
        q = proj_sc[rows, _RQ + hd * D_HEAD:_RQ + (hd + 1) * D_HEAD]
        k = proj_sc[rows, _RK + hd * D_HEAD:_RK + (hd + 1) * D_HEAD]
        v = proj_sc[rows, _RV + hd * D_HEAD:_RV + (hd + 1) * D_HEAD]
        gate = proj_sc[rows, _RG + hd * D_HEAD:_RG + (hd + 1) * D_HEAD]
        q = q * c["cos"] + pltpu.roll(q, D_HEAD // 2, 1) * c["sin"]
        k = (k * c["cos"] + pltpu.roll(k, D_HEAD // 2, 1) * c["sin"]) * (D_HEAD ** -0.5)
        vb = v.astype(_BF16)
        attn = _dot_nt(q.astype(_BF16), k.astype(_BF16)) * c["ret_dmat"][hd]
        s_old = sret_sc[i, hd]
        q_dec = jnp.exp(log_g * (tposf + 1.0))
        k_dec = jnp.exp(log_g * (float(t - 1) - tposf))
        o = _dot(attn.astype(_BF16), vb) + _dot((q * q_dec).astype(_BF16), s_old.astype(_BF16))
        sret_sc[i, hd] = s_old * math.exp(log_g * t) + _dot_tn((k * k_dec).astype(_BF16), vb)
        o = o * lax.rsqrt(jnp.mean(o * o, axis=-1, keepdims=True) + EPS)
        o = o * retg_ref[:, hc] * _silu(gate)
        merged_sc[rows, hc] = o.astype(_BF16)

    def hg_gates(i, d):
        rows = rows_of(i)
        lb = c["lb"]
        f = lb + (1.0 - lb) * _sigmoid(proj_sc[rows, _HF:_HF + GROUP_W])
        g = jnp.log(f)
        d["g"] = g
        d["kk"] = 1.0 - f
        d["hq"] = proj_sc[rows, _HQ:_HQ + GROUP_W]
        g1, g2, g3 = _split3(g)
        tri = c["tri"]
        d["b"] = _dot(tri, g1) + _dot(tri, g2) + _dot(tri, g3)

    def hg_levels(i, d):
        rows = rows_of(i)
        g, b, hq, kk, tpos = d["g"], d["b"], d["hq"], d["kk"], c["tpos"]
        qs_sc[0, rows, :] = hq.astype(_BF16)
        ks_sc[0, rows, :] = kk.astype(_BF16)
        for li, s in enumerate(levels):
            nblk = t // (2 * s)
            if s >= 8:
                b3 = b.reshape(nblk, 2 * s, GROUP_W)
                hq3 = hq.reshape(nblk, 2 * s, GROUP_W)
                kk3 = kk.reshape(nblk, 2 * s, GROUP_W)
                ref_row = b3[:, s - 1:s, :]
                k_top = kk3[:, :s, :] * jnp.exp(ref_row - b3[:, :s, :])
                q_bot = hq3[:, s:, :] * jnp.exp(b3[:, s:, :] - ref_row)
                zeros = jnp.zeros((nblk, s, GROUP_W), _F32)
                qs_sc[li + 1, rows, :] = jnp.concatenate(
                    [zeros, q_bot], axis=1).reshape(t, GROUP_W).astype(_BF16)
                ks_sc[li + 1, rows, :] = jnp.concatenate(
                    [k_top, zeros], axis=1).reshape(t, GROUP_W).astype(_BF16)
                continue
            if s == 4:
                b3 = b.reshape(nblk, 2 * s, GROUP_W)
                ref_row = b3[:, s - 1:s, :]
                e = jnp.exp(-jnp.abs(b3 - ref_row)).reshape(t, GROUP_W)
            elif s == 2:
                g_prev = pltpu.roll(g, 1, 0)
                g_next = pltpu.roll(g, t - 1, 0)
                m4 = jnp.bitwise_and(tpos, 3)
                dd = jnp.where(m4 == 0, g_next, jnp.where(m4 == 1, 0.0, jnp.where(m4 == 2, g, g + g_prev)))
                e = jnp.exp(dd)
            else:
                e = jnp.exp(jnp.where(jnp.bitwise_and(tpos, 1) == 1, g, 0.0))
            qs_sc[li + 1, rows, :] = (hq * e).astype(_BF16)
            ks_sc[li + 1, rows, :] = (kk * e).astype(_BF16)

    def hg_head(i, hd, d):
        rows = rows_of(i)
        pair_level = c["pair_level"]
        hc = slice(hd * D_HEAD, (hd + 1) * D_HEAD)
        p = _dot_nt(qs_sc[0, rows, hc], ks_sc[0, rows, hc]).astype(_BF16)
        a = jnp.where(pair_level == -1.0, p, jnp.zeros_like(p))
        for li, s in enumerate(levels):
            p = _dot_nt(qs_sc[li + 1, rows, hc], ks_sc[li + 1, rows, hc]).astype(_BF16)
            a = jnp.where(pair_level == float(int(math.log2(s))), p, a)
        bh = d["b"][:, hc]
        vb = proj_sc[rows, _HI + hd * D_HEAD:_HI + (hd + 1) * D_HEAD].astype(_BF16)
        st_old = shgt_sc[i, hd]
        o = _dot(a, vb) + _dot_nt((d["hq"][:, hc] * jnp.exp(bh)).astype(_BF16), st_old.astype(_BF16))
        b_last = bh[t - 1:t, :]
        ke = d["kk"][:, hc] * jnp.exp(b_last - bh)
        shgt_sc[i, hd] = st_old * jnp.exp(b_last) + _dot_tn(vb, ke.astype(_BF16))
        oh_sc[rows, hc] = o

    def hg_out(i):
        rows = rows_of(i)
        oh = _rmsnorm(oh_sc[rows, :], hgg_ref[...])
        oh = oh * _silu(proj_sc[rows, _HGATE:_HGATE + GROUP_W])
        merged_sc[rows, GROUP_W:2 * GROUP_W] = oh.astype(_BF16)

    def tail():
        emit_x1(x_ref[...] + _dot(merged_sc[...], w_out_ref[...]))

    proj_pieces = [norm_in] + [functools.partial(proj, cg) for cg in range(8)]
    if nb == 1:
        d = {}
        attn_pieces = [setup]
        attn_pieces += [functools.partial(ret_head, 0, hd) for hd in range(N_HEADS)]
        attn_pieces += [functools.partial(hg_gates, 0, d), functools.partial(hg_levels, 0, d)]
        attn_pieces += [functools.partial(hg_head, 0, hd, d) for hd in range(N_HEADS)]
        attn_pieces += [functools.partial(hg_out, 0)]
    else:
        def all_sequences():
            setup()

            def body(i, carry):
                d = {}
                for hd in range(N_HEADS):
                    ret_head(i, hd)
                hg_gates(i, d)
                hg_levels(i, d)
                for hd in range(N_HEADS):
                    hg_head(i, hd, d)
                hg_out(i)
                return carry
            lax.fori_loop(0, nb, body, 0, unroll=2)
        attn_pieces = [all_sequences]
    return proj_pieces, attn_pieces, [tail]


def _load_states(sret0_ref, shg0_ref, sret_sc, shgt_sc, nb):
    sret_sc[...] = sret0_ref[...]
    for i in range(nb):
        for hd in range(N_HEADS):
            shgt_sc[i, hd] = shg0_ref[i, hd].T


def _store_states(sret_ref, shg_ref, sret_sc, shgt_sc, nb):
    sret_ref[...] = sret_sc[...]
    for i in range(nb):
        for hd in range(N_HEADS):
            shg_ref[i, hd] = shgt_sc[i, hd].T


def _mixer_scratch(nb, t):
    r = nb * t
    nlev = len(_levels(t)) + 1
    return [
        pltpu.VMEM((r, IN_COLS), _F32),
        pltpu.VMEM((r, 2 * GROUP_W), _BF16),
        pltpu.VMEM((r, GROUP_W), _F32),
        pltpu.VMEM((nlev, r, GROUP_W), _BF16),
        pltpu.VMEM((nlev, r, GROUP_W), _BF16),
        pltpu.VMEM((nb, N_HEADS, D_HEAD, D_HEAD), _F32),
        pltpu.VMEM((nb, N_HEADS, D_HEAD, D_HEAD), _F32),
    ]


def _ff_chunks():
    step = 2 * _MXU_COLS
    return [(c0, min(c0 + step, D_FF)) for c0 in range(0, D_FF, step)]


def _ffn_pieces(get_x1, get_hb, w_up_ref, w_down_ref, ck_ref, cb_ref, nfin_ref, act_sc, carry_sc,
                emit_y, *, nb, t):
    r = nb * t
    c = {"x2": []}

    def up(c0, c1):
        w = c1 - c0
        cols = slice(c0, c1)
        hb = get_hb()
        a3 = _dot(hb, w_up_ref[:, cols]).reshape(nb, t, w)
        bgate = _dot(hb, w_up_ref[:, D_FF + c0:D_FF + c1])
        c_m2 = carry_sc[:, 0:1, cols]
        c_m1 = carry_sc[:, 1:2, cols]
        tpos = lax.broadcasted_iota(jnp.int32, (nb, _SUBLANES, 1), 1)
        r1 = pltpu.roll(a3, 1, 1)
        r2 = pltpu.roll(a3, 2, 1)
        head1 = jnp.where(tpos == 0, c_m1, r1[:, :_SUBLANES, :])
        head2 = jnp.where(tpos == 0, c_m2, jnp.where(tpos == 1, c_m1, r2[:, :_SUBLANES, :]))
        a_m1 = jnp.concatenate([head1, r1[:, _SUBLANES:, :]], axis=1)
        a_m2 = jnp.concatenate([head2, r2[:, _SUBLANES:, :]], axis=1)
        conv = cb_ref[:, cols] + a_m2 * ck_ref[0:1, cols] + a_m1 * ck_ref[1:2, cols] + a3 * ck_ref[2:3, cols]
        act_sc[:, cols] = (_silu(conv).reshape(r, w) * bgate).astype(_BF16)
        carry_sc[:, :, cols] = a3[:, t - 2:t, :]

    down_w = 2 * _MXU_COLS

    def down(c0):
        cols = slice(c0, c0 + down_w)
        c["x2"].append(get_x1(cols) + _dot(act_sc[...], w_down_ref[:, cols]))

    def tail():
        emit_y(_rmsnorm(jnp.concatenate(c["x2"], axis=-1), nfin_ref[...]))

    up_pieces = [functools.partial(up, c0, c1) for c0, c1 in _ff_chunks()]
    down_pieces = [functools.partial(down, c0) for c0 in range(0, D_MODEL, down_w)]
    return up_pieces, down_pieces, [tail]


def _ffn_scratch(nb, t):
    return [
        pltpu.VMEM((nb * t, D_FF), _BF16),
        pltpu.VMEM((nb, CONV_W - 1, D_FF), _F32),
    ]


def _prompt_kernel(x_ref, cos_t_ref, sin_t_ref, cos_r_ref, sin_r_ref, cos_rs_ref, sin_rs_ref,
                   w_in_ref, w_out_ref, w_up_ref, w_down_ref,
                   nmix_ref, retg_ref, hgg_ref, lbl_ref, nffn_ref, ck_ref, cb_ref, nfin_ref,
                   y_ref, sret_ref, shg_ref, cache_out_ref,
                   proj_sc, merged_sc, oh_sc, qs_sc, ks_sc, sret_sc, shgt_sc, act_sc, carry_sc,
                   x1_sc, hbf_sc, tri_sc, level_sc, dmat_sc, *, t, nt):
    g = pl.program_id(0)
    tm = lax.rem(g, nt)
    tf = lax.rem(g + (nt - 1), nt)

    @pl.when(tm == 0)
    def _():
        sret_sc[...] = jnp.zeros_like(sret_sc)
        shgt_sc[...] = jnp.zeros_like(shgt_sc)

    @pl.when(tf == 0)
    def _():
        carry_sc[...] = jnp.zeros_like(carry_sc)

    @pl.when(g == 0)
    def _():
        tri, pair_level, ret_dmat = _pair_constants(t)
        tri_sc[...] = tri
        level_sc[...] = pair_level
        for hd in range(N_HEADS):
            dmat_sc[hd] = ret_dmat[hd]

    slot = lax.rem(g, 2)

    def rope():
        cos_t = cos_t_ref[pl.ds(tm, 1), :]
        sin_t = sin_t_ref[pl.ds(tm, 1), :]
        cos = cos_t * cos_r_ref[...] - sin_t * sin_r_ref[...]
        sin_signed = sin_t * cos_rs_ref[...] + cos_t * sin_rs_ref[...]
        return cos, sin_signed

    def emit_x1(x1):
        x1_sc[slot] = x1
        hbf_sc[slot] = _rmsnorm(x1, nffn_ref[...]).astype(_BF16)

    def emit_y(y):
        y_ref[...] = y

    def run(with_mixer, with_ffn):
        m_proj, m_attn, m_tail = _mixer_pieces(
            x_ref, rope, w_in_ref, w_out_ref, nmix_ref, retg_ref, hgg_ref, lbl_ref,
            proj_sc, merged_sc, oh_sc, qs_sc, ks_sc, sret_sc, shgt_sc, emit_x1, nb=1, t=t,
            pair_consts=(tri_sc, level_sc, dmat_sc))
        f_up, f_down, f_tail = _ffn_pieces(
            lambda cols: x1_sc[1 - slot, :, cols], lambda: hbf_sc[1 - slot], w_up_ref, w_down_ref,
            ck_ref, cb_ref, nfin_ref, act_sc, carry_sc, emit_y, nb=1, t=t)
        if not with_ffn:
            order = m_proj + m_attn + m_tail
        elif not with_mixer:
            order = f_up + f_down + f_tail
        else:
            order = m_proj + _interleave(m_attn + m_tail, f_up + f_down[:1]) + f_down[1:] + f_tail
        for piece in order:
            piece()

    last = pl.num_programs(0) - 1
    pl.when(g == 0)(functools.partial(run, True, False))
    pl.when(jnp.logical_and(g > 0, g < last))(functools.partial(run, True, True))
    pl.when(g == last)(functools.partial(run, False, True))

    @pl.when(tm == nt - 1)
    def _():
        _store_states(sret_ref, shg_ref, sret_sc, shgt_sc, 1)

    @pl.when(jnp.logical_and(tf == nt - 1, g > 0))
    def _():
        cache_out_ref[...] = carry_sc[...]


def _const_spec(shape, ngrid):
    if ngrid == 1:
        imap = lambda g: (0,) * len(shape)
    else:
        imap = lambda bi, ti: (0,) * len(shape)
    return pl.BlockSpec(shape, imap, pipeline_mode=pl.Buffered(1))


def _rope_sum_tables(seqlen, t):
    half = D_HEAD // 2
    inv = 1.0 / (ROPE_BASE ** (jnp.arange(half, dtype=_F32) / half))
    dup = lambda x: jnp.concatenate([x, x], axis=-1)
    ang_t = (jnp.arange(seqlen // t, dtype=_F32) * t)[:, None] * inv[None, :]
    ang_r = jnp.arange(t, dtype=_F32)[:, None] * inv[None, :]
    sign = jnp.concatenate([-jnp.ones((half,), _F32), jnp.ones((half,), _F32)])[None, :]
    cos_r, sin_r = dup(jnp.cos(ang_r)), dup(jnp.sin(ang_r))
    return dup(jnp.cos(ang_t)), dup(jnp.sin(ang_t)), cos_r, sin_r, sign * cos_r, sign * sin_r


def _prompt_pass(x2d, weights, *, nseq, seqlen, t):
    (w_in, w_out, w_up, w_down, nmix, retg, hgg, lbl, nffn, ck, cb, nfin) = weights
    nt = seqlen // t
    ntiles = nseq * nt
    mtile = lambda g: jnp.minimum(g, ntiles - 1)
    mseq = lambda g: jnp.minimum(g // nt, nseq - 1)
    ftile = lambda g: jnp.maximum(g - 1, 0)
    fseq = lambda g: jnp.maximum(g - 1, 0) // nt
    cspec = functools.partial(_const_spec, ngrid=1)
    rope_tables = _rope_sum_tables(seqlen, t)
    return pl.pallas_call(
        functools.partial(_prompt_kernel, t=t, nt=nt),
        grid=(ntiles + 1,),
        in_specs=[
            pl.BlockSpec((t, D_MODEL), lambda g: (mtile(g), 0)),
            cspec((nt, D_HEAD)),
            cspec((nt, D_HEAD)),
            cspec((t, D_HEAD)),
            cspec((t, D_HEAD)),
            cspec((t, D_HEAD)),
            cspec((t, D_HEAD)),
            cspec((D_MODEL, IN_COLS)),
            cspec((2 * GROUP_W, D_MODEL)),
            cspec((D_MODEL, 2 * D_FF)),
            cspec((D_FF, D_MODEL)),
            cspec((1, D_MODEL)),
            cspec((1, GROUP_W)),
            cspec((1, GROUP_W)),
            cspec((2, GROUP_W)),
            cspec((1, D_MODEL)),
            cspec((CONV_W, D_FF)),
            cspec((1, D_FF)),
            cspec((1, D_MODEL)),
        ],
        out_specs=[
            pl.BlockSpec((t, D_MODEL), lambda g: (ftile(g), 0)),
            pl.BlockSpec((1, N_HEADS, D_HEAD, D_HEAD), lambda g: (mseq(g), 0, 0, 0)),
            pl.BlockSpec((1, N_HEADS, D_HEAD, D_HEAD), lambda g: (mseq(g), 0, 0, 0)),
            pl.BlockSpec((1, CONV_W - 1, D_FF), lambda g: (fseq(g), 0, 0)),
        ],
        out_shape=[
            jax.ShapeDtypeStruct((nseq * seqlen, D_MODEL), _F32),
            jax.ShapeDtypeStruct((nseq, N_HEADS, D_HEAD, D_HEAD), _F32),
            jax.ShapeDtypeStruct((nseq, N_HEADS, D_HEAD, D_HEAD), _F32),
            jax.ShapeDtypeStruct((nseq, CONV_W - 1, D_FF), _F32),
        ],
        scratch_shapes=_mixer_scratch(1, t) + _ffn_scratch(1, t) + [
            pltpu.VMEM((2, t, D_MODEL), _F32),
            pltpu.VMEM((2, t, D_MODEL), _BF16),
            pltpu.VMEM((t, t), _BF16),
            pltpu.VMEM((t, t), _BF16),
            pltpu.VMEM((N_HEADS, t, t), _F32),
        ],
        compiler_params=pltpu.CompilerParams(
            dimension_semantics=("arbitrary",),
            vmem_limit_bytes=_VMEM_LIMIT_BYTES),
        name="prompt_trunk",
    )(x2d, *rope_tables, w_in, w_out, w_up, w_down,
      nmix, retg, hgg, lbl, nffn, ck, cb, nfin)


def _mixer_kernel(x_ref, cos_ref, sin_ref, sret0_ref, shg0_ref, w_in_ref, w_out_ref,
                  nmix_ref, retg_ref, hgg_ref, lbl_ref,
                  x1_ref, sret_ref, shg_ref,
                  proj_sc, merged_sc, oh_sc, qs_sc, ks_sc, sret_sc, shgt_sc, *, nb, t):
    _load_states(sret0_ref, shg0_ref, sret_sc, shgt_sc, nb)

    def emit_x1(x1):
        x1_ref[...] = x1

    m_proj, m_attn, m_tail = _mixer_pieces(
        x_ref, lambda: (cos_ref[...], sin_ref[...]), w_in_ref, w_out_ref, nmix_ref, retg_ref, hgg_ref,
        lbl_ref, proj_sc, merged_sc, oh_sc, qs_sc, ks_sc, sret_sc, shgt_sc, emit_x1, nb=nb, t=t)
    for piece in m_proj + m_attn + m_tail:
        piece()
    _store_states(sret_ref, shg_ref, sret_sc, shgt_sc, nb)


def _ffn_kernel(x1_ref, cache_ref, w_up_ref, w_down_ref, nffn_ref, ck_ref, cb_ref, nfin_ref,
                y_ref, cache_out_ref, act_sc, carry_sc, *, nb, t):
    carry_sc[...] = cache_ref[...]
    hb = _rmsnorm(x1_ref[...], nffn_ref[...]).astype(_BF16)

    def emit_y(y):
        y_ref[...] = y

    f_up, f_down, f_tail = _ffn_pieces(
        lambda cols: x1_ref[:, cols], lambda: hb, w_up_ref, w_down_ref, ck_ref, cb_ref, nfin_ref,
        act_sc, carry_sc, emit_y, nb=nb, t=t)
    for piece in f_up + f_down + f_tail:
        piece()
    cache_out_ref[...] = carry_sc[...]


def _sample_pass(x2d, cos2, sin2, sret0, shg0, cache, weights, *, nb, t):
    (w_in, w_out, w_up, w_down, nmix, retg, hgg, lbl, nffn, ck, cb, nfin) = weights
    r = nb * t
    full = functools.partial(_const_spec, ngrid=1)
    whole = lambda shape: pl.BlockSpec(shape, lambda g: (0,) * len(shape))
    state_shape = (nb, N_HEADS, D_HEAD, D_HEAD)
    cache_shape = (nb, CONV_W - 1, D_FF)
    params = pltpu.CompilerParams(dimension_semantics=("arbitrary",), vmem_limit_bytes=_VMEM_LIMIT_BYTES)
    x1, ret_new, hg_new = pl.pallas_call(
        functools.partial(_mixer_kernel, nb=nb, t=t),
        grid=(1,),
        in_specs=[full((r, D_MODEL)), full((t, D_HEAD)), full((t, D_HEAD)), full(state_shape),
                  full(state_shape), full((D_MODEL, IN_COLS)), full((2 * GROUP_W, D_MODEL)),
                  full((1, D_MODEL)), full((1, GROUP_W)), full((1, GROUP_W)), full((2, GROUP_W))],
        out_specs=[whole((r, D_MODEL)), whole(state_shape), whole(state_shape)],
        out_shape=[jax.ShapeDtypeStruct((r, D_MODEL), _F32),
                   jax.ShapeDtypeStruct(state_shape, _F32),
                   jax.ShapeDtypeStruct(state_shape, _F32)],
        scratch_shapes=_mixer_scratch(nb, t),
        compiler_params=params,
        name="sample_mixer",
    )(x2d, cos2, sin2, sret0, shg0, w_in, w_out, nmix, retg, hgg, lbl)
    y, conv_new = pl.pallas_call(
        functools.partial(_ffn_kernel, nb=nb, t=t),
        grid=(1,),
        in_specs=[full((r, D_MODEL)), full(cache_shape), full((D_MODEL, 2 * D_FF)), full((D_FF, D_MODEL)),
                  full((1, D_MODEL)), full((CONV_W, D_FF)), full((1, D_FF)), full((1, D_MODEL))],
        out_specs=[whole((r, D_MODEL)), whole(cache_shape)],
        out_shape=[jax.ShapeDtypeStruct((r, D_MODEL), _F32),
                   jax.ShapeDtypeStruct(cache_shape, _F32)],
        scratch_shapes=_ffn_scratch(nb, t),
        compiler_params=params,
        name="sample_convffn",
    )(x1, cache, w_up, w_down, nffn, ck, cb, nfin)
    return y, ret_new, hg_new, conv_new


def _rope_tables(pos):
    half = D_HEAD // 2
    inv = 1.0 / (ROPE_BASE ** (jnp.arange(half, dtype=_F32) / half))
    ang = pos.astype(_F32)[:, None] * inv[None, :]
    cos, sin = jnp.cos(ang), jnp.sin(ang)
    return jnp.concatenate([cos, cos], axis=-1), jnp.concatenate([-sin, sin], axis=-1)


def kernel(x_prompt, x_sample, state_ret, state_hgrn, cache_ffn_conv, w_in, ret_norm_g, hg_norm_g,
           hg_lb_logits, w_out, norm_mix, norm_ffn, w_up, conv_k, conv_b, w_down, norm_final):
    assert w_in.shape[0] == 1, "single-layer trunk"
    weights = (
        w_in[0].astype(_BF16), w_out[0].astype(_BF16), w_up[0].astype(_BF16), w_down[0].astype(_BF16),
        norm_mix[0][None].astype(_F32), ret_norm_g[0][None].astype(_F32), hg_norm_g[0][None].astype(_F32),
        hg_lb_logits.astype(_F32), norm_ffn[0][None].astype(_F32), conv_k[0].astype(_F32),
        conv_b[0][None].astype(_F32), norm_final[None].astype(_F32),
    )
    nseq_p, seq_p, _ = x_prompt.shape
    nseq_s, seq_s, _ = x_sample.shape

    y_p, ret_p, hg_p, conv_p = _prompt_pass(
        x_prompt.reshape(nseq_p * seq_p, D_MODEL), weights, nseq=nseq_p, seqlen=seq_p, t=_PROMPT_TILE)

    cos_s, sin_s = _rope_tables(PAST_LEN + jnp.arange(seq_s, dtype=jnp.int32))
    y_s, ret_s, hg_s, conv_s = _sample_pass(
        x_sample.reshape(nseq_s * seq_s, D_MODEL), cos_s, sin_s, state_ret[0].astype(_F32),
        state_hgrn[0].astype(_F32), cache_ffn_conv[0].astype(_F32), weights, nb=nseq_s, t=seq_s)

    return (y_p.reshape(nseq_p, seq_p, D_MODEL), y_s.reshape(nseq_s, seq_s, D_MODEL),
            ret_p[None], hg_p[None], conv_p[None], ret_s[None], hg_s[None], conv_s[None])
```
